```python
import jax, jax.numpy as jnp
from jax import lax
import numpy as np

D_MODEL = 2048
BATCH = 1
SEQ = 16384
DEPTH = 4
DEC_BATCH = 16
DEC_SEQ = 2048
PAST_LEN = 128

N_MIXERS = 2
N_A_LAYERS = (DEPTH + 1) // 2
N_B_LAYERS = DEPTH // 2
CONV_WIDTH = 3
CONV_DIM = D_MODEL
N_FGROUPS = 8
FGROUP_DIM = D_MODEL // N_FGROUPS
D_FF = 4 * D_MODEL
RMS_EPS = 1e-6

kernel_name = "hybrid_conv_fourier_encoder"


def _rmsnorm(x, g):
    xf = x.astype(jnp.float32)
    y = xf * lax.rsqrt(jnp.mean(xf * xf, axis=-1, keepdims=True) + RMS_EPS)
    return (y * g.astype(jnp.float32)).astype(x.dtype)


def _short_conv_mixer(h, w_in, conv_w, w_out):
    s = h.shape[1]
    b_gate, c_gate, v = jnp.split(h @ w_in, 3, axis=-1)
    u = c_gate * v
    u_pad = jnp.pad(u, ((0, 0), (1, 1), (0, 0)))
    conv = (conv_w[0] * u_pad[:, 0:s]
            + conv_w[1] * u_pad[:, 1:s + 1]
            + conv_w[2] * u_pad[:, 2:s + 2])
    return (b_gate * conv) @ w_out


def _fourier_mixer(h, w_out):
    b, s, d = h.shape
    hg = h.astype(jnp.float32).reshape(b, s, N_FGROUPS, FGROUP_DIM)
    mixed = jnp.fft.fftn(hg, axes=(1, 3), norm="ortho").real
    return mixed.reshape(b, s, d).astype(h.dtype) @ w_out


def _mlp(h, w_up, w_down):
    return jnp.square(jax.nn.relu(h @ w_up)) @ w_down


def _trunk(x, norm_mix, a_w_in, a_conv_w, a_w_out, f_w_out, norm_ffn, w_up, w_down, final_norm):
    for i in range(DEPTH):
        h = _rmsnorm(x, norm_mix[i])
        j = i // N_MIXERS
        if i % N_MIXERS == 0:
            x = x + _short_conv_mixer(h, a_w_in[j], a_conv_w[j], a_w_out[j])
        else:
            x = x + _fourier_mixer(h, f_w_out[j])
        x = x + _mlp(_rmsnorm(x, norm_ffn[i]), w_up[i], w_down[i])
    return _rmsnorm(x, final_norm)


def setup_inputs(seed: int = 0) -> dict:
    key = jax.random.key(seed)
    ks = jax.random.split(key, 12)
    f32 = jnp.float32
    d = D_MODEL
    x_prompt = jax.random.normal(ks[0], (BATCH, SEQ, d), f32)
    x_sample = jax.random.normal(ks[1], (DEC_BATCH, DEC_SEQ, d), f32)
    norm_mix = 1.0 + 0.02 * jax.random.normal(ks[2], (DEPTH, d), f32)
    a_w_in = jax.random.normal(ks[3], (N_A_LAYERS, d, 3 * CONV_DIM), f32) * d ** -0.5
    a_conv_w = jax.random.normal(ks[4], (N_A_LAYERS, CONV_WIDTH, CONV_DIM), f32) * CONV_WIDTH ** -0.5
    a_w_out = jax.random.normal(ks[5], (N_A_LAYERS, CONV_DIM, d), f32) * CONV_DIM ** -0.5
    f_w_out = jax.random.normal(ks[6], (N_B_LAYERS, d, d), f32) * d ** -0.5
    norm_ffn = 1.0 + 0.02 * jax.random.normal(ks[7], (DEPTH, d), f32)
    w_up = jax.random.normal(ks[8], (DEPTH, d, D_FF), f32) * d ** -0.5
    w_down = jax.random.normal(ks[9], (DEPTH, D_FF, d), f32) * D_FF ** -0.5
    final_norm = 1.0 + 0.02 * jax.random.normal(ks[10], (d,), f32)
    return {"x_prompt": x_prompt, "x_sample": x_sample, "norm_mix": norm_mix,
            "a_w_in": a_w_in, "a_conv_w": a_conv_w, "a_w_out": a_w_out,
            "f_w_out": f_w_out, "norm_ffn": norm_ffn, "w_up": w_up,
            "w_down": w_down, "final_norm": final_norm}


def reference(x_prompt, x_sample, norm_mix, a_w_in, a_conv_w, a_w_out, f_w_out,
              norm_ffn, w_up, w_down, final_norm):
    y_prompt = _trunk(x_prompt, norm_mix, a_w_in, a_conv_w, a_w_out, f_w_out,
                      norm_ffn, w_up, w_down, final_norm)
    y_sample = _trunk(x_sample, norm_mix, a_w_in, a_conv_w, a_w_out, f_w_out,
                      norm_ffn, w_up, w_down, final_norm)
    return (y_prompt, y_sample)
```

```python
import functools

import jax
import jax.numpy as jnp
import numpy as np
from jax import lax
from jax.experimental import pallas as pl
from jax.experimental.pallas import tpu as pltpu

F32 = jnp.float32
BF16 = jnp.bfloat16

RMS_EPS = 1e-6
N_FGROUPS = 8
LANES = 128
BF16_ROWS = 16
VMEM_LIMIT_BYTES = 60 * 1024 * 1024


def _params(*sem):
    return pltpu.CompilerParams(dimension_semantics=sem,
                                vmem_limit_bytes=VMEM_LIMIT_BYTES)


def _rms(x, g):
    ms = jnp.mean(x * x, axis=-1, keepdims=True)
    return (x * lax.rsqrt(ms + RMS_EPS)) * g


def _mlp_body(x_ref, g_ref, wu_ref, wd_ref, o_ref, h_ref):
    @pl.when(pl.program_id(1) == 0)
    def _():
        x = x_ref[...]
        h_ref[...] = _rms(x, g_ref[...]).astype(BF16)
        o_ref[...] = x

    u = jnp.dot(h_ref[...], wu_ref[...], preferred_element_type=F32)
    u = jnp.maximum(u, 0.0)
    u = (u * u).astype(BF16)
    o_ref[...] += jnp.dot(u, wd_ref[...], preferred_element_type=F32)


def _mlp(x, g, w_up, w_down, layer, *, tm=512, tf=1024):
    t, d = x.shape
    ff = w_up.shape[-1]
    return pl.pallas_call(
        _mlp_body,
        out_shape=jax.ShapeDtypeStruct((t, d), F32),
        grid=(t // tm, ff // tf),
        in_specs=[
            pl.BlockSpec((tm, d), lambda i, j: (i, 0)),
            pl.BlockSpec((None, 1, d), lambda i, j: (layer, 0, 0)),
            pl.BlockSpec((None, d, tf), lambda i, j: (layer, 0, j)),
            pl.BlockSpec((None, tf, d), lambda i, j: (layer, j, 0)),
        ],
        out_specs=pl.BlockSpec((tm, d), lambda i, j: (i, 0)),
        scratch_shapes=[pltpu.VMEM((tm, d), BF16)],
        compiler_params=_params("parallel", "arbitrary"),
        name="mlp",
    )(x, g, w_up, w_down)


HALO = BF16_ROWS


def _conv_body(seq, tm, x_ref, xp_ref, xn_ref, g_ref, wb_ref, wc_ref, wv_ref,
               cw_ref, wo_ref, o_ref, h_ref):
    i = pl.program_id(0)

    @pl.when(pl.program_id(1) == 0)
    def _():
        g = g_ref[...]
        x = x_ref[...]
        keep_prev = ((i * tm) % seq != 0).astype(F32)
        keep_next = (((i + 1) * tm) % seq != 0).astype(F32)
        h_ref[0:HALO, :] = (_rms(xp_ref[...], g) * keep_prev).astype(BF16)
        h_ref[HALO:HALO + tm, :] = _rms(x, g).astype(BF16)
        h_ref[HALO + tm:, :] = (_rms(xn_ref[...], g) * keep_next).astype(BF16)
        o_ref[...] = x

    rows = tm + 2 * HALO
    h = h_ref[...]
    c = jnp.dot(h, wc_ref[...], preferred_element_type=F32)
    v = jnp.dot(h, wv_ref[...], preferred_element_type=F32)
    b = jnp.dot(h_ref[HALO:HALO + tm, :], wb_ref[...], preferred_element_type=F32)
    u = c * v
    cw = cw_ref[...]
    u_prev = pltpu.roll(u, 1, axis=0)[HALO:HALO + tm]
    u_next = pltpu.roll(u, rows - 1, axis=0)[HALO:HALO + tm]
    conv = cw[0:1] * u_prev + cw[1:2] * u[HALO:HALO + tm] + cw[2:3] * u_next
    gated = (b * conv).astype(BF16)
    o_ref[...] += jnp.dot(gated, wo_ref[...], preferred_element_type=F32)


def _conv_mixer(x, seq, g, w_in, conv_w, w_out, layer, *, tm=512, tc=512):
    t, d = x.shape
    cdim = conv_w.shape[-1]
    nc = cdim // tc
    hb = tm // HALO
    last = t // HALO - 1
    return pl.pallas_call(
        functools.partial(_conv_body, seq, tm),
        out_shape=jax.ShapeDtypeStruct((t, d), F32),
        grid=(t // tm, nc),
        in_specs=[
            pl.BlockSpec((tm, d), lambda i, j: (i, 0)),
            pl.BlockSpec((HALO, d), lambda i, j: (jnp.maximum(i * hb - 1, 0), 0)),
            pl.BlockSpec((HALO, d), lambda i, j: (jnp.minimum((i + 1) * hb, last), 0)),
            pl.BlockSpec((None, 1, d), lambda i, j: (2 * layer, 0, 0)),
            pl.BlockSpec((None, d, tc), lambda i, j: (layer, 0, j)),
            pl.BlockSpec((None, d, tc), lambda i, j: (layer, 0, nc + j)),
            pl.BlockSpec((None, d, tc), lambda i, j: (layer, 0, 2 * nc + j)),
            pl.BlockSpec((None, 3, tc), lambda i, j: (layer, 0, j)),
            pl.BlockSpec((None, tc, d), lambda i, j: (layer, j, 0)),
        ],
        out_specs=pl.BlockSpec((tm, d), lambda i, j: (i, 0)),
        scratch_shapes=[pltpu.VMEM((tm + 2 * HALO, d), BF16)],
        compiler_params=_params("parallel", "arbitrary"),
        name="conv_mixer",
    )(x, x, x, g, w_in, w_in, w_in, conv_w, w_out)


def _norm_body(x_ref, g_ref, o_ref):
    o_ref[...] = _rms(x_ref[...], g_ref[...]).astype(o_ref.dtype)


def _norm(x, g, layer, dtype, *, tm=512):
    t, d = x.shape
    return pl.pallas_call(
        _norm_body,
        out_shape=jax.ShapeDtypeStruct((t, d), dtype),
        grid=(t // tm,),
        in_specs=[pl.BlockSpec((tm, d), lambda i: (i, 0)),
                  pl.BlockSpec((None, 1, d), lambda i: (layer, 0, 0))],
        out_specs=pl.BlockSpec((tm, d), lambda i: (i, 0)),
        compiler_params=_params("parallel"),
        name="rmsnorm",
    )(x, g)


def _dft_tables(seq, n1, n2, kb, cb, gdim):
    c = np.arange(gdim)
    ang = 2.0 * np.pi * ((c[:, None] * c[None, :]) % gdim) / gdim
    fre, fim = np.cos(ang) / np.sqrt(gdim), -np.sin(ang) / np.sqrt(gdim)
    nh = gdim // cb
    fc = np.stack([np.concatenate([fre[:, q * cb:(q + 1) * cb],
                                   fim[:, q * cb:(q + 1) * cb]], axis=1)
                   for q in range(nh)])

    k = np.arange(n1)
    ang = 2.0 * np.pi * ((k[:, None] * k[None, :]) % n1) / n1
    gr, gi = np.cos(ang) / np.sqrt(n1), -np.sin(ang) / np.sqrt(n1)
    f1 = np.block([[gr, -gi], [gi, gr]])

    na = n1 // kb
    m = kb * n2
    h2 = np.zeros((na, m, 2 * m))
    nn = np.arange(n2)
    for a in range(na):
        for i in range(kb):
            k1 = a * kb + i
            kk = k1 + n1 * np.arange(n2)
            ang = 2.0 * np.pi * ((kk[:, None] * nn[None, :]) % seq) / seq
            wr, wi = np.cos(ang) / np.sqrt(n2), -np.sin(ang) / np.sqrt(n2)
            rows = np.arange(n2) * kb + i
            cols = i * n2 + nn
            h2[a, rows[:, None], cols[None, :]] = wr
            h2[a, rows[:, None], m + cols[None, :]] = -wi
    return (np.asarray(fc, np.float32), np.asarray(f1, np.float32),
            np.asarray(h2, np.float32))


def _fft_body(seq, n1, n2, kb, cb, h_ref, fc_ref, f1_ref, h2_ref, o_ref, a_ref):
    nl = cb // LANES
    rc = min(seq, 512)

    def chan(r, carry):
        rows = pl.ds(pl.multiple_of(r * rc, rc), rc)
        w = jnp.dot(h_ref[rows, :], fc_ref[...], preferred_element_type=F32)
        for p in range(2):
            for s in range(nl):
                lo = p * cb + s * LANES
                a_ref[p, s, rows, :] = w[:, lo:lo + LANES]
        return carry
    lax.fori_loop(0, seq // rc, chan, 0)

    def load(rows):
        return jnp.concatenate(
            [jnp.concatenate([a_ref[p, s, rows, :] for s in range(nl)], axis=1)
             for p in range(2)], axis=0).astype(BF16)

    def stage1(j, carry):
        rows = pl.ds(j, n1, stride=n2)
        y = jnp.dot(f1_ref[...], load(rows), preferred_element_type=F32)
        for p in range(2):
            for s in range(nl):
                a_ref[p, s, rows, :] = y[p * n1:(p + 1) * n1, s * LANES:(s + 1) * LANES]
        return carry
    lax.fori_loop(0, n2, stage1, 0)

    m = kb * n2
    def stage2(a, carry):
        rows = pl.ds(pl.multiple_of(a * m, m), m)
        y = jnp.dot(h2_ref[a], load(rows), preferred_element_type=F32)
        if kb == 1:
            o_ref[pl.ds(a, n2, stride=n1), :] = y
        else:
            for k2 in range(n2):
                dst = pl.ds(pl.multiple_of(a * kb, kb) + k2 * n1, kb)
                o_ref[dst, :] = y[k2 * kb:(k2 + 1) * kb]
        return carry
    lax.fori_loop(0, n1 // kb, stage2, 0)


def _fourier(h, batch, seq, *, n1, n2, kb, cb):
    t, d = h.shape
    gdim = d // N_FGROUPS
    nh = gdim // cb
    fc, f1, h2 = (jnp.asarray(tab).astype(BF16)
                  for tab in _dft_tables(seq, n1, n2, kb, cb, gdim))
    nl = cb // LANES
    out = pl.pallas_call(
        functools.partial(_fft_body, seq, n1, n2, kb, cb),
        out_shape=jax.ShapeDtypeStruct((batch, seq, d), F32),
        grid=(batch, d // cb),
        in_specs=[
            pl.BlockSpec((None, seq, gdim), lambda b, l: (b, 0, l // nh),
                         pipeline_mode=pl.Buffered(1)),
            pl.BlockSpec((None, gdim, 2 * cb), lambda b, l: (l % nh, 0, 0)),
            pl.BlockSpec(f1.shape, lambda b, l: (0, 0), pipeline_mode=pl.Buffered(1)),
            pl.BlockSpec(h2.shape, lambda b, l: (0, 0, 0), pipeline_mode=pl.Buffered(1)),
        ],
        out_specs=pl.BlockSpec((None, seq, cb), lambda b, l: (b, 0, l)),
        scratch_shapes=[pltpu.VMEM((2, nl, seq, LANES), F32)],
        compiler_params=_params("parallel", "arbitrary"),
        name="fourier",
    )(h.reshape(batch, seq, d), fc, f1, h2)
    return out.reshape(t, d)


def _proj_body(x_ref, m_ref, w_ref, o_ref):
    o_ref[...] = x_ref[...] + jnp.dot(m_ref[...].astype(BF16), w_ref[...],
                                      preferred_element_type=F32)


def _proj_residual(x, mixed, w, layer, *, tm=512):
    t, d = x.shape
    return pl.pallas_call(
        _proj_body,
        out_shape=jax.ShapeDtypeStruct((t, d), F32),
        grid=(t // tm,),
        in_specs=[pl.BlockSpec((tm, d), lambda i: (i, 0)),
                  pl.BlockSpec((tm, d), lambda i: (i, 0)),
                  pl.BlockSpec((None, d, d), lambda i: (layer, 0, 0),
                               pipeline_mode=pl.Buffered(1))],
        out_specs=pl.BlockSpec((tm, d), lambda i: (i, 0)),
        compiler_params=_params("parallel"),
        name="proj_residual",
    )(x, mixed, w)


_FFT_CONFIG = {16384: dict(n1=128, n2=128, kb=1, cb=128),
               2048: dict(n1=128, n2=16, kb=8, cb=256)}


def _trunk(x3, w):
    batch, seq, d = x3.shape
    x = x3.reshape(batch * seq, d)
    depth = w["norm_ffn"].shape[0]
    for i in range(depth):
        j = i // 2
        if i % 2 == 0:
            x = _conv_mixer(x, seq, w["norm_mix"], w["a_w_in"], w["a_conv_w"],
                            w["a_w_out"], j)
        else:
            h = _norm(x, w["norm_mix"], i, BF16)
            mixed = _fourier(h, batch, seq, **_FFT_CONFIG[seq])
            x = _proj_residual(x, mixed, w["f_w_out"], j)
        x = _mlp(x, w["norm_ffn"], w["w_up"], w["w_down"], i)
    y = _norm(x, w["final_norm"], 0, F32)
    return y.reshape(batch, seq, d)


def kernel(x_prompt, x_sample, norm_mix, a_w_in, a_conv_w, a_w_out, f_w_out,
           norm_ffn, w_up, w_down, final_norm):
    d = x_prompt.shape[-1]
    w = dict(
        norm_mix=norm_mix.reshape(-1, 1, d),
        norm_ffn=norm_ffn.reshape(-1, 1, d),
        final_norm=final_norm.reshape(1, 1, d),
        a_w_in=a_w_in.astype(BF16),
        a_conv_w=a_conv_w,
        a_w_out=a_w_out.astype(BF16),
        f_w_out=f_w_out.astype(BF16),
        w_up=w_up.astype(BF16),
        w_down=w_down.astype(BF16),
    )
    return _trunk(x_prompt, w), _trunk(x_sample, w)
```

```python
import functools
from typing import NamedTuple

import jax
import jax.numpy as jnp
import numpy as np
from jax import lax
from jax.experimental import pallas as pl
from jax.experimental.pallas import tpu as pltpu

F32 = jnp.float32
BF16 = jnp.bfloat16

RMS_EPS = 1e-6
N_FGROUPS = 8
LANES = 128
BF16_ROWS = 16
VMEM_LIMIT_BYTES = 60 * 1024 * 1024


def _params(*sem):
    return pltpu.CompilerParams(dimension_semantics=sem,
                                vmem_limit_bytes=VMEM_LIMIT_BYTES)


def _rms(x, g):
    ms = jnp.mean(x * x, axis=-1, keepdims=True)
    return (x * lax.rsqrt(ms + RMS_EPS)) * g


def _mlp_body(x_ref, g_ref, wu_ref, wd_ref, o_ref, h_ref):
    @pl.when(pl.program_id(1) == 0)
    def _():
        x = x_ref[...]
        h_ref[...] = _rms(x, g_ref[...]).astype(BF16)
        o_ref[...] = x

    u = jnp.dot(h_ref[...], wu_ref[...], preferred_element_type=F32)
    u = jnp.maximum(u, 0.0)
    u = (u * u).astype(BF16)
    o_ref[...] += jnp.dot(u, wd_ref[...], preferred_element_type=F32)


def _mlp(x, g, w_up, w_down, layer, *, tm=512, tf=1024):
    t, d = x.shape
    ff = w_up.shape[-1]
    return pl.pallas_call(
        _mlp_body,
        out_shape=jax.ShapeDtypeStruct((t, d), F32),
        grid=(t // tm, ff // tf),
        in_specs=[
            pl.BlockSpec((tm, d), lambda i, j: (i, 0)),
            pl.BlockSpec((None, 1, d), lambda i, j: (layer, 0, 0)),
            pl.BlockSpec((None, d, tf), lambda i, j: (layer, 0, j)),
            pl.BlockSpec((None, tf, d), lambda i, j: (layer, j, 0)),
        ],
        out_specs=pl.BlockSpec((tm, d), lambda i, j: (i, 0)),
        scratch_shapes=[pltpu.VMEM((tm, d), BF16)],
        compiler_params=_params("parallel", "arbitrary"),
        name="mlp",
    )(x, g, w_up, w_down)


HALO = BF16_ROWS


def _conv_body(seq, tm, x_ref, xp_ref, xn_ref, g_ref, wb_ref, wc_ref, wv_ref,
               cw_ref, wo_ref, o_ref, h_ref):
    i = pl.program_id(0)

    @pl.when(pl.program_id(1) == 0)
    def _():
        g = g_ref[...]
        x = x_ref[...]
        keep_prev = ((i * tm) % seq != 0).astype(F32)
        keep_next = (((i + 1) * tm) % seq != 0).astype(F32)
        h_ref[0:HALO, :] = (_rms(xp_ref[...], g) * keep_prev).astype(BF16)
        h_ref[HALO:HALO + tm, :] = _rms(x, g).astype(BF16)
        h_ref[HALO + tm:, :] = (_rms(xn_ref[...], g) * keep_next).astype(BF16)
        o_ref[...] = x

    rows = tm + 2 * HALO
    h = h_ref[...]
    c = jnp.dot(h, wc_ref[...], preferred_element_type=F32)
    v = jnp.dot(h, wv_ref[...], preferred_element_type=F32)
    b = jnp.dot(h_ref[HALO:HALO + tm, :], wb_ref[...], preferred_element_type=F32)
    u = c * v
    cw = cw_ref[...]
    u_prev = pltpu.roll(u, 1, axis=0)[HALO:HALO + tm]
    u_next = pltpu.roll(u, rows - 1, axis=0)[HALO:HALO + tm]
    conv = cw[0:1] * u_prev + cw[1:2] * u[HALO:HALO + tm] + cw[2:3] * u_next
    gated = (b * conv).astype(BF16)
    o_ref[...] += jnp.dot(gated, wo_ref[...], preferred_element_type=F32)


def _conv_mixer(x, seq, g, w_in, conv_w, w_out, layer, *, tm=512, tc=512):
    t, d = x.shape
    cdim = conv_w.shape[-1]
    nc = cdim // tc
    hb = tm // HALO
    last = t // HALO - 1
    return pl.pallas_call(
        functools.partial(_conv_body, seq, tm),
        out_shape=jax.ShapeDtypeStruct((t, d), F32),
        grid=(t // tm, nc),
        in_specs=[
            pl.BlockSpec((tm, d), lambda i, j: (i, 0)),
            pl.BlockSpec((HALO, d), lambda i, j: (jnp.maximum(i * hb - 1, 0), 0)),
            pl.BlockSpec((HALO, d), lambda i, j: (jnp.minimum((i + 1) * hb, last), 0)),
            pl.BlockSpec((None, 1, d), lambda i, j: (2 * layer, 0, 0)),
            pl.BlockSpec((None, d, tc), lambda i, j: (layer, 0, j)),
            pl.BlockSpec((None, d, tc), lambda i, j: (layer, 0, nc + j)),
            pl.BlockSpec((None, d, tc), lambda i, j: (layer, 0, 2 * nc + j)),
            pl.BlockSpec((None, 3, tc), lambda i, j: (layer, 0, j)),
            pl.BlockSpec((None, tc, d), lambda i, j: (layer, j, 0)),
        ],
        out_specs=pl.BlockSpec((tm, d), lambda i, j: (i, 0)),
        scratch_shapes=[pltpu.VMEM((tm + 2 * HALO, d), BF16)],
        compiler_params=_params("parallel", "arbitrary"),
        name="conv_mixer",
    )(x, x, x, g, w_in, w_in, w_in, conv_w, w_out)


def _norm_body(x_ref, g_ref, o_ref):
    o_ref[...] = _rms(x_ref[...], g_ref[...]).astype(o_ref.dtype)


def _norm(x, g, layer, dtype, *, tm=512):
    t, d = x.shape
    return pl.pallas_call(
        _norm_body,
        out_shape=jax.ShapeDtypeStruct((t, d), dtype),
        grid=(t // tm,),
        in_specs=[pl.BlockSpec((tm, d), lambda i: (i, 0)),
                  pl.BlockSpec((None, 1, d), lambda i: (layer, 0, 0))],
        out_specs=pl.BlockSpec((tm, d), lambda i: (i, 0)),
        compiler_params=_params("parallel"),
        name="rmsnorm",
    )(x, g)


def _dft_tables(seq, n1, n2, kb, cb, gdim):
    c = np.arange(gdim)
    ang = 2.0 * np.pi * ((c[:, None] * c[None, :]) % gdim) / gdim
    fre, fim = np.cos(ang) / np.sqrt(gdim), -np.sin(ang) / np.sqrt(gdim)
    nh = gdim // cb
    fc = np.stack([np.concatenate([fre[:, q * cb:(q + 1) * cb],
                                   fim[:, q * cb:(q + 1) * cb]], axis=1)
                   for q in range(nh)])

    k = np.arange(n1)
    ang = 2.0 * np.pi * ((k[:, None] * k[None, :]) % n1) / n1
    gr, gi = np.cos(ang) / np.sqrt(n1), -np.sin(ang) / np.sqrt(n1)
    f1 = np.block([[gr, -gi], [gi, gr]])

    na = n1 // kb
    m = kb * n2
    h2 = np.zeros((na, m, 2 * m))
    nn = np.arange(n2)
    for a in range(na):
        for i in range(kb):
            k1 = a * kb + i
            kk = k1 + n1 * np.arange(n2)
            ang = 2.0 * np.pi * ((kk[:, None] * nn[None, :]) % seq) / seq
            wr, wi = np.cos(ang) / np.sqrt(n2), -np.sin(ang) / np.sqrt(n2)
            rows = np.arange(n2) * kb + i
            cols = i * n2 + nn
            h2[a, rows[:, None], cols[None, :]] = wr
            h2[a, rows[:, None], m + cols[None, :]] = -wi
    return (np.asarray(fc, np.float32), np.asarray(f1, np.float32),
            np.asarray(h2, np.float32))


class FftCfg(NamedTuple):
    seq: int
    n1: int
    n2: int
    kb: int
    cb: int
    pitch_a: int
    pitch_o: int
    unroll: int


def _loop(n, body, unroll):
    if unroll >= n:
        for i in range(n):
            body(i)
    else:
        lax.fori_loop(0, n, lambda i, c: (body(i), c)[1], 0, unroll=unroll)


def _aligned(x, m):
    return x if isinstance(x, int) else pl.multiple_of(x, m)


def _fft_body(cfg, h_ref, fc_ref, f1_ref, h2_ref, o_ref, a_ref):
    seq, n1, n2, kb, cb, pa, po, unroll = cfg
    nl = cb // LANES
    rc = max(n2, min(seq, 512))
    per = rc // n2

    def chan(r):
        w = jnp.dot(h_ref[pl.ds(_aligned(r * rc, rc), rc), :], fc_ref[...],
                    preferred_element_type=F32)
        for q in range(per if pa != n2 else 1):
            size = n2 if pa != n2 else rc
            dst = pl.ds(_aligned((r * per + q) * pa, 8), size)
            for p in range(2):
                for s in range(nl):
                    lo = p * cb + s * LANES
                    a_ref[p, s, dst, :] = w[q * n2:q * n2 + size, lo:lo + LANES]
    _loop(seq // rc, chan, unroll)

    def load(rows):
        return jnp.concatenate(
            [jnp.concatenate([a_ref[p, s, rows, :] for s in range(nl)], axis=1)
             for p in range(2)], axis=0).astype(BF16)

    nb = max(1, 2 * LANES // cb)
    def stage1(t):
        rows = [pl.ds(t * nb + q, n1, stride=pa) for q in range(nb)]
        rhs = jnp.concatenate([load(r) for r in rows], axis=1)
        y = jnp.dot(f1_ref[...], rhs, preferred_element_type=F32)
        for q in range(nb):
            for p in range(2):
                for s in range(nl):
                    lo = q * cb + s * LANES
                    a_ref[p, s, rows[q], :] = y[p * n1:(p + 1) * n1, lo:lo + LANES]
    _loop(n2 // nb, stage1, unroll)

    m = kb * n2
    def stage2(a):
        y = jnp.dot(h2_ref[a], load(pl.ds(_aligned(a * kb * pa, 8), m)),
                    preferred_element_type=F32)
        if kb == 1:
            o_ref[pl.ds(a, n2, stride=po), :] = y
        else:
            for k2 in range(n2):
                o_ref[pl.ds(_aligned(a * kb, kb) + k2 * po, kb), :] = (
                    y[k2 * kb:(k2 + 1) * kb])
    _loop(n1 // kb, stage2, unroll)

    if po != n1:
        for k2 in range(n2):
            o_ref[k2 * po + n1:(k2 + 1) * po, :] = jnp.zeros((po - n1, cb), F32)


def _fourier(h, batch, cfg):
    t, d = h.shape
    seq, n1, n2, kb, cb, pa, po, _ = cfg
    assert kb == 1 or pa == n2
    gdim = d // N_FGROUPS
    nh = gdim // cb
    fc, f1, h2 = (jnp.asarray(tab).astype(BF16)
                  for tab in _dft_tables(seq, n1, n2, kb, cb, gdim))
    out = pl.pallas_call(
        functools.partial(_fft_body, cfg),
        out_shape=jax.ShapeDtypeStruct((batch, n2 * po, d), F32),
        grid=(batch, d // cb),
        in_specs=[
            pl.BlockSpec((None, seq, gdim), lambda b, l: (b, 0, l // nh),
                         pipeline_mode=pl.Buffered(1)),
            pl.BlockSpec((None, gdim, 2 * cb), lambda b, l: (l % nh, 0, 0)),
            pl.BlockSpec(f1.shape, lambda b, l: (0, 0), pipeline_mode=pl.Buffered(1)),
            pl.BlockSpec(h2.shape, lambda b, l: (0, 0, 0), pipeline_mode=pl.Buffered(1)),
        ],
        out_specs=pl.BlockSpec((None, n2 * po, cb), lambda b, l: (b, 0, l)),
        scratch_shapes=[pltpu.VMEM((2, cb // LANES, n1 * pa, LANES), F32)],
        compiler_params=_params("parallel", "arbitrary"),
        name="fourier",
    )(h.reshape(batch, seq, d), fc, f1, h2)
    return out.reshape(batch * n2 * po, d)


def _proj_body(n1, po, x_ref, m_ref, w_ref, o_ref):
    if po == n1:
        m = m_ref[...]
    else:
        m = jnp.concatenate([m_ref[q * po:q * po + n1, :]
                             for q in range(m_ref.shape[0] // po)], axis=0)
    o_ref[...] = x_ref[...] + jnp.dot(m.astype(BF16), w_ref[...],
                                      preferred_element_type=F32)


def _proj_residual(x, mixed, cfg, w, layer, *, tm=512):
    t, d = x.shape
    mrows = tm // cfg.n1 * cfg.pitch_o
    return pl.pallas_call(
        functools.partial(_proj_body, cfg.n1, cfg.pitch_o),
        out_shape=jax.ShapeDtypeStruct((t, d), F32),
        grid=(t // tm,),
        in_specs=[pl.BlockSpec((tm, d), lambda i: (i, 0)),
                  pl.BlockSpec((mrows, d), lambda i: (i, 0)),
                  pl.BlockSpec((None, d, d), lambda i: (layer, 0, 0),
                               pipeline_mode=pl.Buffered(1))],
        out_specs=pl.BlockSpec((tm, d), lambda i: (i, 0)),
        compiler_params=_params("parallel"),
        name="proj_residual",
    )(x, mixed, w)


_FFT_CONFIG = {
    16384: FftCfg(seq=16384, n1=128, n2=128, kb=1, cb=128, pitch_a=136,
                  pitch_o=136, unroll=4),
    2048: FftCfg(seq=2048, n1=128, n2=16, kb=8, cb=256, pitch_a=16,
                 pitch_o=128, unroll=16),
}


def _trunk(x3, w):
    batch, seq, d = x3.shape
    x = x3.reshape(batch * seq, d)
    depth = w["norm_ffn"].shape[0]
    for i in range(depth):
        j = i // 2
        if i % 2 == 0:
            x = _conv_mixer(x, seq, w["norm_mix"], w["a_w_in"], w["a_conv_w"],
                            w["a_w_out"], j)
        else:
            cfg = _FFT_CONFIG[seq]
            h = _norm(x, w["norm_mix"], i, BF16)
            mixed = _fourier(h, batch, cfg)
            x = _proj_residual(x, mixed, cfg, w["f_w_out"], j)
        x = _mlp(x, w["norm_ffn"], w["w_up"], w["w_down"], i)
    y = _norm(x, w["final_norm"], 0, F32)
    return y.reshape(batch, seq, d)


def kernel(x_prompt, x_sample, norm_mix, a_w_in, a_conv_w, a_w_out, f_w_out,
           norm_ffn, w_up, w_down, final_norm):
    d = x_prompt.shape[-1]
    w = dict(
        norm_mix=norm_mix.reshape(-1, 1, d),
        norm_ffn=norm_ffn.reshape(-1, 1, d),
        final_norm=final_norm.reshape(1, 1, d),
        a_w_in=a_w_in.astype(BF16),
        a_conv_w=a_conv_w,
        a_w_out=a_w_out.astype(BF16),
        f_w_out=f_w_out.astype(BF16),
        w_up=w_up.astype(BF16),
        w_down=w_down.astype(BF16),
    )
    return _trunk(x_prompt, w), _trunk(x_sample, w)
```

```python
import functools
from typing import NamedTuple

import jax
import jax.numpy as jnp
import numpy as np
from jax import lax
from jax.experimental import pallas as pl
from jax.experimental.pallas import tpu as pltpu

F32 = jnp.float32
BF16 = jnp.bfloat16

RMS_EPS = 1e-6
N_FGROUPS = 8
LANES = 128
BF16_ROWS = 16
VMEM_LIMIT_BYTES = 60 * 1024 * 1024


def _params(*sem):
    return pltpu.CompilerParams(dimension_semantics=sem,
                                vmem_limit_bytes=VMEM_LIMIT_BYTES)


def _rms(x, g):
    ms = jnp.mean(x * x, axis=-1, keepdims=True)
    return (x * lax.rsqrt(ms + RMS_EPS)) * g


def _mlp_body(post, x_ref, g_ref, wu_ref, wd_ref, *rest):
    if post is None:
        acc_ref, h_ref = rest
    elif post == "also":
        g2_ref, acc_ref, n_ref, h_ref = rest
    else:
        g2_ref, n_ref, h_ref, acc_ref = rest
    j = pl.program_id(1)

    @pl.when(j == 0)
    def _():
        x = x_ref[...]
        h_ref[...] = _rms(x, g_ref[...]).astype(BF16)
        acc_ref[...] = x

    u = jnp.dot(h_ref[...], wu_ref[...], preferred_element_type=F32)
    u = jnp.maximum(u, 0.0)
    u = (u * u).astype(BF16)
    acc_ref[...] += jnp.dot(u, wd_ref[...], preferred_element_type=F32)

    if post is not None:
        @pl.when(j == pl.num_programs(1) - 1)
        def _():
            n_ref[...] = _rms(acc_ref[...], g2_ref[...]).astype(n_ref.dtype)


def _mlp(x, g, w_up, w_down, layer, post=None, g2=None, layer2=0, *, tm=512, tf=1024):
    t, d = x.shape
    ff = w_up.shape[-1]
    tile = pl.BlockSpec((tm, d), lambda i, j: (i, 0))
    in_specs = [
        tile,
        pl.BlockSpec((None, 1, d), lambda i, j: (layer, 0, 0)),
        pl.BlockSpec((None, d, tf), lambda i, j: (layer, 0, j)),
        pl.BlockSpec((None, tf, d), lambda i, j: (layer, j, 0)),
    ]
    args = [x, g, w_up, w_down]
    scratch = [pltpu.VMEM((tm, d), BF16)]
    x_new = jax.ShapeDtypeStruct((t, d), F32)
    if post is None:
        out_shape, out_specs = x_new, tile
    else:
        in_specs.append(pl.BlockSpec((None, 1, d), lambda i, j: (layer2, 0, 0)))
        args.append(g2)
        if post == "also":
            out_shape = (x_new, jax.ShapeDtypeStruct((t, d), BF16))
            out_specs = (tile, tile)
        else:
            out_shape, out_specs = x_new, tile
            scratch.append(pltpu.VMEM((tm, d), F32))
    return pl.pallas_call(
        functools.partial(_mlp_body, post),
        out_shape=out_shape,
        grid=(t // tm, ff // tf),
        in_specs=in_specs,
        out_specs=out_specs,
        scratch_shapes=scratch,
        compiler_params=_params("parallel", "arbitrary"),
        name="mlp",
    )(*args)


HALO = BF16_ROWS


def _conv_body(seq, tm, x_ref, xp_ref, xn_ref, g_ref, wb_ref, wc_ref, wv_ref,
               cw_ref, wo_ref, o_ref, h_ref):
    i = pl.program_id(0)

    @pl.when(pl.program_id(1) == 0)
    def _():
        g = g_ref[...]
        x = x_ref[...]
        keep_prev = ((i * tm) % seq != 0).astype(F32)
        keep_next = (((i + 1) * tm) % seq != 0).astype(F32)
        h_ref[0:HALO, :] = (_rms(xp_ref[...], g) * keep_prev).astype(BF16)
        h_ref[HALO:HALO + tm, :] = _rms(x, g).astype(BF16)
        h_ref[HALO + tm:, :] = (_rms(xn_ref[...], g) * keep_next).astype(BF16)
        o_ref[...] = x

    rows = tm + 2 * HALO
    h = h_ref[...]
    c = jnp.dot(h, wc_ref[...], preferred_element_type=F32)
    v = jnp.dot(h, wv_ref[...], preferred_element_type=F32)
    b = jnp.dot(h_ref[HALO:HALO + tm, :], wb_ref[...], preferred_element_type=F32)
    u = c * v
    cw = cw_ref[...]
    u_prev = pltpu.roll(u, 1, axis=0)[HALO:HALO + tm]
    u_next = pltpu.roll(u, rows - 1, axis=0)[HALO:HALO + tm]
    conv = cw[0:1] * u_prev + cw[1:2] * u[HALO:HALO + tm] + cw[2:3] * u_next
    gated = (b * conv).astype(BF16)
    o_ref[...] += jnp.dot(gated, wo_ref[...], preferred_element_type=F32)


def _conv_mixer(x, seq, g, w_in, conv_w, w_out, layer, *, tm=512, tc=512):
    t, d = x.shape
    cdim = conv_w.shape[-1]
    nc = cdim // tc
    hb = tm // HALO
    last = t // HALO - 1
    return pl.pallas_call(
        functools.partial(_conv_body, seq, tm),
        out_shape=jax.ShapeDtypeStruct((t, d), F32),
        grid=(t // tm, nc),
        in_specs=[
            pl.BlockSpec((tm, d), lambda i, j: (i, 0)),
            pl.BlockSpec((HALO, d), lambda i, j: (jnp.maximum(i * hb - 1, 0), 0)),
            pl.BlockSpec((HALO, d), lambda i, j: (jnp.minimum((i + 1) * hb, last), 0)),
            pl.BlockSpec((None, 1, d), lambda i, j: (2 * layer, 0, 0)),
            pl.BlockSpec((None, d, tc), lambda i, j: (layer, 0, j)),
            pl.BlockSpec((None, d, tc), lambda i, j: (layer, 0, nc + j)),
            pl.BlockSpec((None, d, tc), lambda i, j: (layer, 0, 2 * nc + j)),
            pl.BlockSpec((None, 3, tc), lambda i, j: (layer, 0, j)),
            pl.BlockSpec((None, tc, d), lambda i, j: (layer, j, 0)),
        ],
        out_specs=pl.BlockSpec((tm, d), lambda i, j: (i, 0)),
        scratch_shapes=[pltpu.VMEM((tm + 2 * HALO, d), BF16)],
        compiler_params=_params("parallel", "arbitrary"),
        name="conv_mixer",
    )(x, x, x, g, w_in, w_in, w_in, conv_w, w_out)


def _dft_tables(seq, n1, n2, kb, cb, gdim):
    c = np.arange(gdim)
    ang = 2.0 * np.pi * ((c[:, None] * c[None, :]) % gdim) / gdim
    fre, fim = np.cos(ang) / np.sqrt(gdim), -np.sin(ang) / np.sqrt(gdim)
    nh = gdim // cb
    fc = np.stack([np.concatenate([fre[:, q * cb:(q + 1) * cb],
                                   fim[:, q * cb:(q + 1) * cb]], axis=1)
                   for q in range(nh)])

    k = np.arange(n1)
    ang = 2.0 * np.pi * ((k[:, None] * k[None, :]) % n1) / n1
    gr, gi = np.cos(ang) / np.sqrt(n1), -np.sin(ang) / np.sqrt(n1)
    f1 = np.block([[gr, -gi], [gi, gr]])

    na = n1 // kb
    m = kb * n2
    h2 = np.zeros((na, m, 2 * m))
    nn = np.arange(n2)
    for a in range(na):
        for i in range(kb):
            k1 = a * kb + i
            kk = k1 + n1 * np.arange(n2)
            ang = 2.0 * np.pi * ((kk[:, None] * nn[None, :]) % seq) / seq
            wr, wi = np.cos(ang) / np.sqrt(n2), -np.sin(ang) / np.sqrt(n2)
            rows = np.arange(n2) * kb + i
            cols = i * n2 + nn
            h2[a, rows[:, None], cols[None, :]] = wr
            h2[a, rows[:, None], m + cols[None, :]] = -wi
    return (np.asarray(fc, np.float32), np.asarray(f1, np.float32),
            np.asarray(h2, np.float32))


class FftCfg(NamedTuple):
    seq: int
    n1: int
    n2: int
    kb: int
    cb: int
    pitch_a: int
    pitch_o: int
    unroll: int
    h_buffers: int


def _loop(n, body, unroll):
    if unroll >= n:
        for i in range(n):
            body(i)
    else:
        lax.fori_loop(0, n, lambda i, c: (body(i), c)[1], 0, unroll=unroll)


def _aligned(x, m):
    return x if isinstance(x, int) else pl.multiple_of(x, m)


def _fft_body(cfg, h_ref, fc_ref, f1_ref, h2_ref, o_ref, a_ref):
    seq, n1, n2, kb, cb, pa, po, unroll, _ = cfg
    nl = cb // LANES
    rc = max(n2, min(seq, 512))
    per = rc // n2

    def chan(r):
        w = jnp.dot(h_ref[pl.ds(_aligned(r * rc, rc), rc), :], fc_ref[...],
                    preferred_element_type=F32)
        for q in range(per if pa != n2 else 1):
            size = n2 if pa != n2 else rc
            dst = pl.ds(_aligned((r * per + q) * pa, 8), size)
            for p in range(2):
                for s in range(nl):
                    lo = p * cb + s * LANES
                    a_ref[p, s, dst, :] = w[q * n2:q * n2 + size, lo:lo + LANES]
    _loop(seq // rc, chan, unroll)

    def load(rows):
        return jnp.concatenate(
            [jnp.concatenate([a_ref[p, s, rows, :] for s in range(nl)], axis=1)
             for p in range(2)], axis=0).astype(BF16)

    nb = max(1, 2 * LANES // cb)
    def stage1(t):
        rows = [pl.ds(t * nb + q, n1, stride=pa) for q in range(nb)]
        rhs = jnp.concatenate([load(r) for r in rows], axis=1)
        y = jnp.dot(f1_ref[...], rhs, preferred_element_type=F32)
        for q in range(nb):
            for p in range(2):
                for s in range(nl):
                    lo = q * cb + s * LANES
                    a_ref[p, s, rows[q], :] = y[p * n1:(p + 1) * n1, lo:lo + LANES]
    _loop(n2 // nb, stage1, unroll)

    m = kb * n2
    def stage2(a):
        y = jnp.dot(h2_ref[a], load(pl.ds(_aligned(a * kb * pa, 8), m)),
                    preferred_element_type=F32)
        if kb == 1:
            o_ref[pl.ds(a, n2, stride=po), :] = y
        else:
            for k2 in range(n2):
                o_ref[pl.ds(_aligned(a * kb, kb) + k2 * po, kb), :] = (
                    y[k2 * kb:(k2 + 1) * kb])
    _loop(n1 // kb, stage2, unroll)

    if po != n1:
        for k2 in range(n2):
            o_ref[k2 * po + n1:(k2 + 1) * po, :] = jnp.zeros((po - n1, cb), F32)


def _fourier(h, batch, cfg):
    t, d = h.shape
    seq, n1, n2, kb, cb, pa, po, _, _ = cfg
    assert kb == 1 or pa == n2
    gdim = d // N_FGROUPS
    nh = gdim // cb
    fc, f1, h2 = (jnp.asarray(tab).astype(BF16)
                  for tab in _dft_tables(seq, n1, n2, kb, cb, gdim))
    out = pl.pallas_call(
        functools.partial(_fft_body, cfg),
        out_shape=jax.ShapeDtypeStruct((batch, n2 * po, d), F32),
        grid=(batch, d // cb),
        in_specs=[
            pl.BlockSpec((None, seq, gdim), lambda b, l: (b, 0, l // nh),
                         pipeline_mode=pl.Buffered(cfg.h_buffers)),
            pl.BlockSpec((None, gdim, 2 * cb), lambda b, l: (l % nh, 0, 0)),
            pl.BlockSpec(f1.shape, lambda b, l: (0, 0), pipeline_mode=pl.Buffered(1)),
            pl.BlockSpec(h2.shape, lambda b, l: (0, 0, 0), pipeline_mode=pl.Buffered(1)),
        ],
        out_specs=pl.BlockSpec((None, n2 * po, cb), lambda b, l: (b, 0, l)),
        scratch_shapes=[pltpu.VMEM((2, cb // LANES, n1 * pa, LANES), F32)],
        compiler_params=_params("parallel", "arbitrary"),
        name="fourier",
    )(h.reshape(batch, seq, d), fc, f1, h2)
    return out.reshape(batch * n2 * po, d)


def _proj_body(n1, po, x_ref, m_ref, w_ref, o_ref):
    if po == n1:
        m = m_ref[...]
    else:
        m = jnp.concatenate([m_ref[q * po:q * po + n1, :]
                             for q in range(m_ref.shape[0] // po)], axis=0)
    o_ref[...] = x_ref[...] + jnp.dot(m.astype(BF16), w_ref[...],
                                      preferred_element_type=F32)


def _proj_residual(x, mixed, cfg, w, layer, *, tm=512):
    t, d = x.shape
    mrows = tm // cfg.n1 * cfg.pitch_o
    return pl.pallas_call(
        functools.partial(_proj_body, cfg.n1, cfg.pitch_o),
        out_shape=jax.ShapeDtypeStruct((t, d), F32),
        grid=(t // tm,),
        in_specs=[pl.BlockSpec((tm, d), lambda i: (i, 0)),
                  pl.BlockSpec((mrows, d), lambda i: (i, 0)),
                  pl.BlockSpec((None, d, d), lambda i: (layer, 0, 0),
                               pipeline_mode=pl.Buffered(1))],
        out_specs=pl.BlockSpec((tm, d), lambda i: (i, 0)),
        compiler_params=_params("parallel"),
        name="proj_residual",
    )(x, mixed, w)


_FFT_CONFIG = {
    16384: FftCfg(seq=16384, n1=128, n2=128, kb=1, cb=128, pitch_a=136,
                  pitch_o=136, unroll=4, h_buffers=1),
    2048: FftCfg(seq=2048, n1=128, n2=16, kb=8, cb=256, pitch_a=16,
                 pitch_o=128, unroll=16, h_buffers=2),
}


def _trunk(x3, w):
    batch, seq, d = x3.shape
    x = x3.reshape(batch * seq, d)
    depth = w["norm_ffn"].shape[0]
    h = None
    for i in range(depth):
        j = i // 2
        if i % 2 == 0:
            x = _conv_mixer(x, seq, w["norm_mix"], w["a_w_in"], w["a_conv_w"],
                            w["a_w_out"], j)
        else:
            cfg = _FFT_CONFIG[seq]
            mixed = _fourier(h, batch, cfg)
            x = _proj_residual(x, mixed, cfg, w["f_w_out"], j)
        mlp = functools.partial(_mlp, x, w["norm_ffn"], w["w_up"], w["w_down"], i)
        if i == depth - 1:
            x = mlp("only", w["final_norm"], 0)
        elif (i + 1) % 2 == 1:
            x, h = mlp("also", w["norm_mix"], i + 1)
        else:
            x = mlp()
    return x.reshape(batch, seq, d)


def kernel(x_prompt, x_sample, norm_mix, a_w_in, a_conv_w, a_w_out, f_w_out,
           norm_ffn, w_up, w_down, final_norm):
    d = x_prompt.shape[-1]
    w = dict(
        norm_mix=norm_mix.reshape(-1, 1, d),
        norm_ffn=norm_ffn.reshape(-1, 1, d),
        final_norm=final_norm.reshape(1, 1, d),
        a_w_in=a_w_in.astype(BF16),
        a_conv_w=a_conv_w,
        a_w_out=a_w_out.astype(BF16),
        f_w_out=f_w_out.astype(BF16),
        w_up=w_up.astype(BF16),
        w_down=w_down.astype(BF16),
    )
    return _trunk(x_prompt, w), _trunk(x_sample, w)
```

```python
import functools
from typing import NamedTuple

import jax
import jax.numpy as jnp
import numpy as np
from jax import lax
from jax.experimental import pallas as pl
from jax.experimental.pallas import tpu as pltpu

F32 = jnp.float32
BF16 = jnp.bfloat16

RMS_EPS = 1e-6
N_FGROUPS = 8
LANES = 128
BF16_ROWS = 16
VMEM_LIMIT_BYTES = 60 * 1024 * 1024


def _params(*sem):
    return pltpu.CompilerParams(dimension_semantics=sem,
                                vmem_limit_bytes=VMEM_LIMIT_BYTES)


def _rms(x, g):
    ms = jnp.mean(x * x, axis=-1, keepdims=True)
    return (x * lax.rsqrt(ms + RMS_EPS)) * g


def _mlp_body(post, x_ref, g_ref, wu_ref, wd_ref, *rest):
    if post is None:
        acc_ref, h_ref = rest
    elif post == "also":
        g2_ref, acc_ref, n_ref, h_ref = rest
    else:
        g2_ref, n_ref, h_ref, acc_ref = rest
    j = pl.program_id(1)

    @pl.when(j == 0)
    def _():
        x = x_ref[...]
        h_ref[...] = _rms(x, g_ref[...]).astype(BF16)
        acc_ref[...] = x

    u = jnp.dot(h_ref[...], wu_ref[...], preferred_element_type=F32)
    u = jnp.maximum(u, 0.0)
    u = (u * u).astype(BF16)
    acc_ref[...] += jnp.dot(u, wd_ref[...], preferred_element_type=F32)

    if post is not None:
        @pl.when(j == pl.num_programs(1) - 1)
        def _():
            n_ref[...] = _rms(acc_ref[...], g2_ref[...]).astype(n_ref.dtype)


def _mlp(x, g, w_up, w_down, layer, post=None, g2=None, layer2=0, *, tm=1024, tf=512):
    t, d = x.shape
    ff = w_up.shape[-1]
    tile = pl.BlockSpec((tm, d), lambda i, j: (i, 0))
    in_specs = [
        tile,
        pl.BlockSpec((None, 1, d), lambda i, j: (layer, 0, 0)),
        pl.BlockSpec((None, d, tf), lambda i, j: (layer, 0, j)),
        pl.BlockSpec((None, tf, d), lambda i, j: (layer, j, 0)),
    ]
    args = [x, g, w_up, w_down]
    scratch = [pltpu.VMEM((tm, d), BF16)]
    x_new = jax.ShapeDtypeStruct((t, d), F32)
    if post is None:
        out_shape, out_specs = x_new, tile
    else:
        in_specs.append(pl.BlockSpec((None, 1, d), lambda i, j: (layer2, 0, 0)))
        args.append(g2)
        if post == "also":
            out_shape = (x_new, jax.ShapeDtypeStruct((t, d), BF16))
            out_specs = (tile, tile)
        else:
            out_shape, out_specs = x_new, tile
            scratch.append(pltpu.VMEM((tm, d), F32))
    return pl.pallas_call(
        functools.partial(_mlp_body, post),
        out_shape=out_shape,
        grid=(t // tm, ff // tf),
        in_specs=in_specs,
        out_specs=out_specs,
        scratch_shapes=scratch,
        compiler_params=_params("parallel", "arbitrary"),
        name="mlp",
    )(*args)


HALO = BF16_ROWS


def _conv_body(seq, tm, x_ref, xp_ref, xn_ref, g_ref, wb_ref, wc_ref, wv_ref,
               cw_ref, wo_ref, o_ref, h_ref):
    i = pl.program_id(0)

    @pl.when(pl.program_id(1) == 0)
    def _():
        g = g_ref[...]
        x = x_ref[...]
        keep_prev = ((i * tm) % seq != 0).astype(F32)
        keep_next = (((i + 1) * tm) % seq != 0).astype(F32)
        h_ref[0:HALO, :] = (_rms(xp_ref[...], g) * keep_prev).astype(BF16)
        h_ref[HALO:HALO + tm, :] = _rms(x, g).astype(BF16)
        h_ref[HALO + tm:, :] = (_rms(xn_ref[...], g) * keep_next).astype(BF16)
        o_ref[...] = x

    rows = tm + 2 * HALO
    h = h_ref[...]
    c = jnp.dot(h, wc_ref[...], preferred_element_type=F32)
    v = jnp.dot(h, wv_ref[...], preferred_element_type=F32)
    b = jnp.dot(h_ref[HALO:HALO + tm, :], wb_ref[...], preferred_element_type=F32)
    u = c * v
    cw = cw_ref[...]
    u_prev = pltpu.roll(u, 1, axis=0)[HALO:HALO + tm]
    u_next = pltpu.roll(u, rows - 1, axis=0)[HALO:HALO + tm]
    conv = cw[0:1] * u_prev + cw[1:2] * u[HALO:HALO + tm] + cw[2:3] * u_next
    gated = (b * conv).astype(BF16)
    o_ref[...] += jnp.dot(gated, wo_ref[...], preferred_element_type=F32)


def _conv_mixer(x, seq, g, w_in, conv_w, w_out, layer, *, tm=512, tc=512):
    t, d = x.shape
    cdim = conv_w.shape[-1]
    nc = cdim // tc
    hb = tm // HALO
    last = t // HALO - 1
    return pl.pallas_call(
        functools.partial(_conv_body, seq, tm),
        out_shape=jax.ShapeDtypeStruct((t, d), F32),
        grid=(t // tm, nc),
        in_specs=[
            pl.BlockSpec((tm, d), lambda i, j: (i, 0)),
            pl.BlockSpec((HALO, d), lambda i, j: (jnp.maximum(i * hb - 1, 0), 0)),
            pl.BlockSpec((HALO, d), lambda i, j: (jnp.minimum((i + 1) * hb, last), 0)),
            pl.BlockSpec((None, 1, d), lambda i, j: (2 * layer, 0, 0)),
            pl.BlockSpec((None, d, tc), lambda i, j: (layer, 0, j)),
            pl.BlockSpec((None, d, tc), lambda i, j: (layer, 0, nc + j)),
            pl.BlockSpec((None, d, tc), lambda i, j: (layer, 0, 2 * nc + j)),
            pl.BlockSpec((None, 3, tc), lambda i, j: (layer, 0, j)),
            pl.BlockSpec((None, tc, d), lambda i, j: (layer, j, 0)),
        ],
        out_specs=pl.BlockSpec((tm, d), lambda i, j: (i, 0)),
        scratch_shapes=[pltpu.VMEM((tm + 2 * HALO, d), BF16)],
        compiler_params=_params("parallel", "arbitrary"),
        name="conv_mixer",
    )(x, x, x, g, w_in, w_in, w_in, conv_w, w_out)


def _dft_tables(seq, n1, n2, kb, cb, gdim):
    c = np.arange(gdim)
    ang = 2.0 * np.pi * ((c[:, None] * c[None, :]) % gdim) / gdim
    fre, fim = np.cos(ang) / np.sqrt(gdim), -np.sin(ang) / np.sqrt(gdim)
    nh = gdim // cb
    fc = np.stack([np.concatenate([fre[:, q * cb:(q + 1) * cb],
                                   fim[:, q * cb:(q + 1) * cb]], axis=1)
                   for q in range(nh)])

    k = np.arange(n1)
    ang = 2.0 * np.pi * ((k[:, None] * k[None, :]) % n1) / n1
    gr, gi = np.cos(ang) / np.sqrt(n1), -np.sin(ang) / np.sqrt(n1)
    f1 = np.block([[gr, -gi], [gi, gr]])

    na = n1 // kb
    m = kb * n2
    h2 = np.zeros((na, m, 2 * m))
    nn = np.arange(n2)
    for a in range(na):
        for i in range(kb):
            k1 = a * kb + i
            kk = k1 + n1 * np.arange(n2)
            ang = 2.0 * np.pi * ((kk[:, None] * nn[None, :]) % seq) / seq
            wr, wi = np.cos(ang) / np.sqrt(n2), -np.sin(ang) / np.sqrt(n2)
            rows = np.arange(n2) * kb + i
            cols = i * n2 + nn
            h2[a, rows[:, None], cols[None, :]] = wr
            h2[a, rows[:, None], m + cols[None, :]] = -wi
    return (np.asarray(fc, np.float32), np.asarray(f1, np.float32),
            np.asarray(h2, np.float32))


class FftCfg(NamedTuple):
    seq: int
    n1: int
    n2: int
    kb: int
    cb: int
    pitch_a: int
    pitch_o: int
    unroll: int
    h_buffers: int


def _loop(n, body, unroll):
    if unroll >= n:
        for i in range(n):
            body(i)
    else:
        lax.fori_loop(0, n, lambda i, c: (body(i), c)[1], 0, unroll=unroll)


def _aligned(x, m):
    return x if isinstance(x, int) else pl.multiple_of(x, m)


def _fft_body(cfg, h_ref, fc_ref, f1_ref, h2_ref, o_ref, a_ref):
    seq, n1, n2, kb, cb, pa, po, unroll, _ = cfg
    nl = cb // LANES
    rc = max(n2, min(seq, 512))
    per = rc // n2

    def chan(r):
        w = jnp.dot(h_ref[pl.ds(_aligned(r * rc, rc), rc), :], fc_ref[...],
                    preferred_element_type=F32)
        for q in range(per if pa != n2 else 1):
            size = n2 if pa != n2 else rc
            dst = pl.ds(_aligned((r * per + q) * pa, 8), size)
            for p in range(2):
                for s in range(nl):
                    lo = p * cb + s * LANES
                    a_ref[p, s, dst, :] = w[q * n2:q * n2 + size, lo:lo + LANES]
    _loop(seq // rc, chan, unroll)

    def load(rows):
        return jnp.concatenate(
            [jnp.concatenate([a_ref[p, s, rows, :] for s in range(nl)], axis=1)
             for p in range(2)], axis=0).astype(BF16)

    nb = max(1, 2 * LANES // cb)
    def stage1(t):
        rows = [pl.ds(t * nb + q, n1, stride=pa) for q in range(nb)]
        rhs = jnp.concatenate([load(r) for r in rows], axis=1)
        y = jnp.dot(f1_ref[...], rhs, preferred_element_type=F32)
        for q in range(nb):
            for p in range(2):
                for s in range(nl):
                    lo = q * cb + s * LANES
                    a_ref[p, s, rows[q], :] = y[p * n1:(p + 1) * n1, lo:lo + LANES]
    _loop(n2 // nb, stage1, unroll)

    m = kb * n2
    def stage2(a):
        y = jnp.dot(h2_ref[a], load(pl.ds(_aligned(a * kb * pa, 8), m)),
                    preferred_element_type=F32)
        if kb == 1:
            o_ref[pl.ds(a, n2, stride=po), :] = y
        else:
            for k2 in range(n2):
                o_ref[pl.ds(_aligned(a * kb, kb) + k2 * po, kb), :] = (
                    y[k2 * kb:(k2 + 1) * kb])
    _loop(n1 // kb, stage2, unroll)

    if po != n1:
        for k2 in range(n2):
            o_ref[k2 * po + n1:(k2 + 1) * po, :] = jnp.zeros((po - n1, cb), F32)


def _fourier(h, batch, cfg):
    t, d = h.shape
    seq, n1, n2, kb, cb, pa, po, _, _ = cfg
    assert kb == 1 or pa == n2
    gdim = d // N_FGROUPS
    nh = gdim // cb
    fc, f1, h2 = (jnp.asarray(tab).astype(BF16)
                  for tab in _dft_tables(seq, n1, n2, kb, cb, gdim))
    out = pl.pallas_call(
        functools.partial(_fft_body, cfg),
        out_shape=jax.ShapeDtypeStruct((batch, n2 * po, d), F32),
        grid=(batch, d // cb),
        in_specs=[
            pl.BlockSpec((None, seq, gdim), lambda b, l: (b, 0, l // nh),
                         pipeline_mode=pl.Buffered(cfg.h_buffers)),
            pl.BlockSpec((None, gdim, 2 * cb), lambda b, l: (l % nh, 0, 0)),
            pl.BlockSpec(f1.shape, lambda b, l: (0, 0), pipeline_mode=pl.Buffered(1)),
            pl.BlockSpec(h2.shape, lambda b, l: (0, 0, 0), pipeline_mode=pl.Buffered(1)),
        ],
        out_specs=pl.BlockSpec((None, n2 * po, cb), lambda b, l: (b, 0, l)),
        scratch_shapes=[pltpu.VMEM((2, cb // LANES, n1 * pa, LANES), F32)],
        compiler_params=_params("parallel", "arbitrary"),
        name="fourier",
    )(h.reshape(batch, seq, d), fc, f1, h2)
    return out.reshape(batch * n2 * po, d)


def _proj_body(n1, po, x_ref, m_ref, w_ref, o_ref):
    if po == n1:
        m = m_ref[...]
    else:
        m = jnp.concatenate([m_ref[q * po:q * po + n1, :]
                             for q in range(m_ref.shape[0] // po)], axis=0)
    o_ref[...] = x_ref[...] + jnp.dot(m.astype(BF16), w_ref[...],
                                      preferred_element_type=F32)


def _proj_residual(x, mixed, cfg, w, layer, *, tm=512):
    t, d = x.shape
    mrows = tm // cfg.n1 * cfg.pitch_o
    return pl.pallas_call(
        functools.partial(_proj_body, cfg.n1, cfg.pitch_o),
        out_shape=jax.ShapeDtypeStruct((t, d), F32),
        grid=(t // tm,),
        in_specs=[pl.BlockSpec((tm, d), lambda i: (i, 0)),
                  pl.BlockSpec((mrows, d), lambda i: (i, 0)),
                  pl.BlockSpec((None, d, d), lambda i: (layer, 0, 0),
                               pipeline_mode=pl.Buffered(1))],
        out_specs=pl.BlockSpec((tm, d), lambda i: (i, 0)),
        compiler_params=_params("parallel"),
        name="proj_residual",
    )(x, mixed, w)


_FFT_CONFIG = {
    16384: FftCfg(seq=16384, n1=128, n2=128, kb=1, cb=128, pitch_a=136,
                  pitch_o=136, unroll=4, h_buffers=1),
    2048: FftCfg(seq=2048, n1=128, n2=16, kb=8, cb=256, pitch_a=16,
                 pitch_o=128, unroll=16, h_buffers=2),
}


def _trunk(x3, w):
    batch, seq, d = x3.shape
    x = x3.reshape(batch * seq, d)
    depth = w["norm_ffn"].shape[0]
    h = None
    for i in range(depth):
        j = i // 2
        if i % 2 == 0:
            x = _conv_mixer(x, seq, w["norm_mix"], w["a_w_in"], w["a_conv_w"],
                            w["a_w_out"], j)
        else:
            cfg = _FFT_CONFIG[seq]
            mixed = _fourier(h, batch, cfg)
            x = _proj_residual(x, mixed, cfg, w["f_w_out"], j)
        mlp = functools.partial(_mlp, x, w["norm_ffn"], w["w_up"], w["w_down"], i)
        if i == depth - 1:
            x = mlp("only", w["final_norm"], 0)
        elif (i + 1) % 2 == 1:
            x, h = mlp("also", w["norm_mix"], i + 1)
        else:
            x = mlp()
    return x.reshape(batch, seq, d)


def kernel(x_prompt, x_sample, norm_mix, a_w_in, a_conv_w, a_w_out, f_w_out,
           norm_ffn, w_up, w_down, final_norm):
    d = x_prompt.shape[-1]
    w = dict(
        norm_mix=norm_mix.reshape(-1, 1, d),
        norm_ffn=norm_ffn.reshape(-1, 1, d),
        final_norm=final_norm.reshape(1, 1, d),
        a_w_in=a_w_in.astype(BF16),
        a_conv_w=a_conv_w,
        a_w_out=a_w_out.astype(BF16),
        f_w_out=f_w_out.astype(BF16),
        w_up=w_up.astype(BF16),
        w_down=w_down.astype(BF16),
    )
    return _trunk(x_prompt, w), _trunk(x_sample, w)
```

```python
import functools
from typing import NamedTuple

import jax
import jax.numpy as jnp
import numpy as np
from jax import lax
from jax.experimental import pallas as pl
from jax.experimental.pallas import tpu as pltpu

F32 = jnp.float32
BF16 = jnp.bfloat16

RMS_EPS = 1e-6
N_FGROUPS = 8
LANES = 128
BF16_ROWS = 16
VMEM_LIMIT_BYTES = 60 * 1024 * 1024


def _params(*sem):
    return pltpu.CompilerParams(dimension_semantics=sem,
                                vmem_limit_bytes=VMEM_LIMIT_BYTES)


def _rms(x, g):
    ms = jnp.mean(x * x, axis=-1, keepdims=True)
    return (x * lax.rsqrt(ms + RMS_EPS)) * g


def _mlp_body(post, x_ref, g_ref, wu_ref, wd_ref, *rest):
    if post is None:
        acc_ref, h_ref = rest
    elif post == "also":
        g2_ref, acc_ref, n_ref, h_ref = rest
    else:
        g2_ref, n_ref, h_ref, acc_ref = rest
    j = pl.program_id(1)

    @pl.when(j == 0)
    def _():
        x = x_ref[...]
        h_ref[...] = _rms(x, g_ref[...]).astype(BF16)
        acc_ref[...] = x

    u = jnp.dot(h_ref[...], wu_ref[...], preferred_element_type=F32)
    u = jnp.maximum(u, 0.0)
    u = (u * u).astype(BF16)
    acc_ref[...] += jnp.dot(u, wd_ref[...], preferred_element_type=F32)

    if post is not None:
        @pl.when(j == pl.num_programs(1) - 1)
        def _():
            n_ref[...] = _rms(acc_ref[...], g2_ref[...]).astype(n_ref.dtype)


def _mlp(x, g, w_up, w_down, layer, post=None, g2=None, layer2=0, *, tm=512, tf=2048):
    t, d = x.shape
    ff = w_up.shape[-1]
    tile = pl.BlockSpec((tm, d), lambda i, j: (i, 0))
    in_specs = [
        tile,
        pl.BlockSpec((None, 1, d), lambda i, j: (layer, 0, 0)),
        pl.BlockSpec((None, d, tf), lambda i, j: (layer, 0, j)),
        pl.BlockSpec((None, tf, d), lambda i, j: (layer, j, 0)),
    ]
    args = [x, g, w_up, w_down]
    scratch = [pltpu.VMEM((tm, d), BF16)]
    x_new = jax.ShapeDtypeStruct((t, d), F32)
    if post is None:
        out_shape, out_specs = x_new, tile
    else:
        in_specs.append(pl.BlockSpec((None, 1, d), lambda i, j: (layer2, 0, 0)))
        args.append(g2)
        if post == "also":
            out_shape = (x_new, jax.ShapeDtypeStruct((t, d), BF16))
            out_specs = (tile, tile)
        else:
            out_shape, out_specs = x_new, tile
            scratch.append(pltpu.VMEM((tm, d), F32))
    return pl.pallas_call(
        functools.partial(_mlp_body, post),
        out_shape=out_shape,
        grid=(t // tm, ff // tf),
        in_specs=in_specs,
        out_specs=out_specs,
        scratch_shapes=scratch,
        compiler_params=_params("parallel", "arbitrary"),
        name="mlp",
    )(*args)


HALO = BF16_ROWS


def _conv_body(seq, tm, x_ref, xp_ref, xn_ref, g_ref, wb_ref, wc_ref, wv_ref,
               cw_ref, wo_ref, o_ref, h_ref):
    i = pl.program_id(0)

    @pl.when(pl.program_id(1) == 0)
    def _():
        g = g_ref[...]
        x = x_ref[...]
        keep_prev = ((i * tm) % seq != 0).astype(F32)
        keep_next = (((i + 1) * tm) % seq != 0).astype(F32)
        h_ref[0:HALO, :] = (_rms(xp_ref[...], g) * keep_prev).astype(BF16)
        h_ref[HALO:HALO + tm, :] = _rms(x, g).astype(BF16)
        h_ref[HALO + tm:, :] = (_rms(xn_ref[...], g) * keep_next).astype(BF16)
        o_ref[...] = x

    rows = tm + 2 * HALO
    h = h_ref[...]
    c = jnp.dot(h, wc_ref[...], preferred_element_type=F32)
    v = jnp.dot(h, wv_ref[...], preferred_element_type=F32)
    b = jnp.dot(h_ref[HALO:HALO + tm, :], wb_ref[...], preferred_element_type=F32)
    u = c * v
    cw = cw_ref[...]
    u_prev = pltpu.roll(u, 1, axis=0)[HALO:HALO + tm]
    u_next = pltpu.roll(u, rows - 1, axis=0)[HALO:HALO + tm]
    conv = cw[0:1] * u_prev + cw[1:2] * u[HALO:HALO + tm] + cw[2:3] * u_next
    gated = (b * conv).astype(BF16)
    o_ref[...] += jnp.dot(gated, wo_ref[...], preferred_element_type=F32)


def _conv_mixer(x, seq, g, w_in, conv_w, w_out, layer, *, tm=512, tc=512):
    t, d = x.shape
    cdim = conv_w.shape[-1]
    nc = cdim // tc
    hb = tm // HALO
    last = t // HALO - 1
    return pl.pallas_call(
        functools.partial(_conv_body, seq, tm),
        out_shape=jax.ShapeDtypeStruct((t, d), F32),
        grid=(t // tm, nc),
        in_specs=[
            pl.BlockSpec((tm, d), lambda i, j: (i, 0)),
            pl.BlockSpec((HALO, d), lambda i, j: (jnp.maximum(i * hb - 1, 0), 0)),
            pl.BlockSpec((HALO, d), lambda i, j: (jnp.minimum((i + 1) * hb, last), 0)),
            pl.BlockSpec((None, 1, d), lambda i, j: (2 * layer, 0, 0)),
            pl.BlockSpec((None, d, tc), lambda i, j: (layer, 0, j)),
            pl.BlockSpec((None, d, tc), lambda i, j: (layer, 0, nc + j)),
            pl.BlockSpec((None, d, tc), lambda i, j: (layer, 0, 2 * nc + j)),
            pl.BlockSpec((None, 3, tc), lambda i, j: (layer, 0, j)),
            pl.BlockSpec((None, tc, d), lambda i, j: (layer, j, 0)),
        ],
        out_specs=pl.BlockSpec((tm, d), lambda i, j: (i, 0)),
        scratch_shapes=[pltpu.VMEM((tm + 2 * HALO, d), BF16)],
        compiler_params=_params("parallel", "arbitrary"),
        name="conv_mixer",
    )(x, x, x, g, w_in, w_in, w_in, conv_w, w_out)


def _dft_tables(seq, n1, n2, kb, cb, gdim):
    c = np.arange(gdim)
    ang = 2.0 * np.pi * ((c[:, None] * c[None, :]) % gdim) / gdim
    fre, fim = np.cos(ang) / np.sqrt(gdim), -np.sin(ang) / np.sqrt(gdim)
    nh = gdim // cb
    fc = np.stack([np.concatenate([fre[:, q * cb:(q + 1) * cb],
                                   fim[:, q * cb:(q + 1) * cb]], axis=1)
                   for q in range(nh)])

    k = np.arange(n1)
    ang = 2.0 * np.pi * ((k[:, None] * k[None, :]) % n1) / n1
    gr, gi = np.cos(ang) / np.sqrt(n1), -np.sin(ang) / np.sqrt(n1)
    f1 = np.block([[gr, -gi], [gi, gr]])

    na = n1 // kb
    m = kb * n2
    h2 = np.zeros((na, m, 2 * m))
    nn = np.arange(n2)
    for a in range(na):
        for i in range(kb):
            k1 = a * kb + i
            kk = k1 + n1 * np.arange(n2)
            ang = 2.0 * np.pi * ((kk[:, None] * nn[None, :]) % seq) / seq
            wr, wi = np.cos(ang) / np.sqrt(n2), -np.sin(ang) / np.sqrt(n2)
            rows = np.arange(n2) * kb + i
            cols = i * n2 + nn
            h2[a, rows[:, None], cols[None, :]] = wr
            h2[a, rows[:, None], m + cols[None, :]] = -wi
    return (np.asarray(fc, np.float32), np.asarray(f1, np.float32),
            np.asarray(h2, np.float32))


class FftCfg(NamedTuple):
    seq: int
    n1: int
    n2: int
    kb: int
    cb: int
    pitch_a: int
    pitch_o: int
    unroll: int
    h_buffers: int


def _loop(n, body, unroll):
    if unroll >= n:
        for i in range(n):
            body(i)
    else:
        lax.fori_loop(0, n, lambda i, c: (body(i), c)[1], 0, unroll=unroll)


def _aligned(x, m):
    return x if isinstance(x, int) else pl.multiple_of(x, m)


def _fft_body(cfg, h_ref, fc_ref, f1_ref, h2_ref, o_ref, a_ref):
    seq, n1, n2, kb, cb, pa, po, unroll, _ = cfg
    nl = cb // LANES
    rc = max(n2, min(seq, 512))
    per = rc // n2

    def chan(r):
        w = jnp.dot(h_ref[pl.ds(_aligned(r * rc, rc), rc), :], fc_ref[...],
                    preferred_element_type=F32)
        for q in range(per if pa != n2 else 1):
            size = n2 if pa != n2 else rc
            dst = pl.ds(_aligned((r * per + q) * pa, 8), size)
            for p in range(2):
                for s in range(nl):
                    lo = p * cb + s * LANES
                    a_ref[p, s, dst, :] = w[q * n2:q * n2 + size, lo:lo + LANES]
    _loop(seq // rc, chan, unroll)

    def load(rows):
        return jnp.concatenate(
            [jnp.concatenate([a_ref[p, s, rows, :] for s in range(nl)], axis=1)
             for p in range(2)], axis=0).astype(BF16)

    nb = max(1, 2 * LANES // cb)
    def stage1(t):
        rows = [pl.ds(t * nb + q, n1, stride=pa) for q in range(nb)]
        rhs = jnp.concatenate([load(r) for r in rows], axis=1)
        y = jnp.dot(f1_ref[...], rhs, preferred_element_type=F32)
        for q in range(nb):
            for p in range(2):
                for s in range(nl):
                    lo = q * cb + s * LANES
                    a_ref[p, s, rows[q], :] = y[p * n1:(p + 1) * n1, lo:lo + LANES]
    _loop(n2 // nb, stage1, unroll)

    m = kb * n2
    def stage2(a):
        y = jnp.dot(h2_ref[a], load(pl.ds(_aligned(a * kb * pa, 8), m)),
                    preferred_element_type=F32)
        if kb == 1:
            o_ref[pl.ds(a, n2, stride=po), :] = y
        else:
            for k2 in range(n2):
                o_ref[pl.ds(_aligned(a * kb, kb) + k2 * po, kb), :] = (
                    y[k2 * kb:(k2 + 1) * kb])
    _loop(n1 // kb, stage2, unroll)

    if po != n1:
        for k2 in range(n2):
            o_ref[k2 * po + n1:(k2 + 1) * po, :] = jnp.zeros((po - n1, cb), F32)


def _fourier(h, batch, cfg):
    t, d = h.shape
    seq, n1, n2, kb, cb, pa, po, _, _ = cfg
    assert kb == 1 or pa == n2
    gdim = d // N_FGROUPS
    nh = gdim // cb
    fc, f1, h2 = (jnp.asarray(tab).astype(BF16)
                  for tab in _dft_tables(seq, n1, n2, kb, cb, gdim))
    out = pl.pallas_call(
        functools.partial(_fft_body, cfg),
        out_shape=jax.ShapeDtypeStruct((batch, n2 * po, d), F32),
        grid=(batch, d // cb),
        in_specs=[
            pl.BlockSpec((None, seq, gdim), lambda b, l: (b, 0, l // nh),
                         pipeline_mode=pl.Buffered(cfg.h_buffers)),
            pl.BlockSpec((None, gdim, 2 * cb), lambda b, l: (l % nh, 0, 0)),
            pl.BlockSpec(f1.shape, lambda b, l: (0, 0), pipeline_mode=pl.Buffered(1)),
            pl.BlockSpec(h2.shape, lambda b, l: (0, 0, 0), pipeline_mode=pl.Buffered(1)),
        ],
        out_specs=pl.BlockSpec((None, n2 * po, cb), lambda b, l: (b, 0, l)),
        scratch_shapes=[pltpu.VMEM((2, cb // LANES, n1 * pa, LANES), F32)],
        compiler_params=_params("parallel", "arbitrary"),
        name="fourier",
    )(h.reshape(batch, seq, d), fc, f1, h2)
    return out.reshape(batch * n2 * po, d)


def _proj_body(n1, po, x_ref, m_ref, w_ref, o_ref):
    if po == n1:
        m = m_ref[...]
    else:
        m = jnp.concatenate([m_ref[q * po:q * po + n1, :]
                             for q in range(m_ref.shape[0] // po)], axis=0)
    o_ref[...] = x_ref[...] + jnp.dot(m.astype(BF16), w_ref[...],
                                      preferred_element_type=F32)


def _proj_residual(x, mixed, cfg, w, layer, *, tm=512):
    t, d = x.shape
    mrows = tm // cfg.n1 * cfg.pitch_o
    return pl.pallas_call(
        functools.partial(_proj_body, cfg.n1, cfg.pitch_o),
        out_shape=jax.ShapeDtypeStruct((t, d), F32),
        grid=(t // tm,),
        in_specs=[pl.BlockSpec((tm, d), lambda i: (i, 0)),
                  pl.BlockSpec((mrows, d), lambda i: (i, 0)),
                  pl.BlockSpec((None, d, d), lambda i: (layer, 0, 0),
                               pipeline_mode=pl.Buffered(1))],
        out_specs=pl.BlockSpec((tm, d), lambda i: (i, 0)),
        compiler_params=_params("parallel"),
        name="proj_residual",
    )(x, mixed, w)


_FFT_CONFIG = {
    16384: FftCfg(seq=16384, n1=128, n2=128, kb=1, cb=128, pitch_a=136,
                  pitch_o=136, unroll=4, h_buffers=1),
    2048: FftCfg(seq=2048, n1=128, n2=16, kb=8, cb=256, pitch_a=16,
                 pitch_o=128, unroll=16, h_buffers=2),
}


def _trunk(x3, w):
    batch, seq, d = x3.shape
    x = x3.reshape(batch * seq, d)
    depth = w["norm_ffn"].shape[0]
    h = None
    for i in range(depth):
        j = i // 2
        if i % 2 == 0:
            x = _conv_mixer(x, seq, w["norm_mix"], w["a_w_in"], w["a_conv_w"],
                            w["a_w_out"], j)
        else:
            cfg = _FFT_CONFIG[seq]
            mixed = _fourier(h, batch, cfg)
            x = _proj_residual(x, mixed, cfg, w["f_w_out"], j)
        mlp = functools.partial(_mlp, x, w["norm_ffn"], w["w_up"], w["w_down"], i)
        if i == depth - 1:
            x = mlp("only", w["final_norm"], 0)
        elif (i + 1) % 2 == 1:
            x, h = mlp("also", w["norm_mix"], i + 1)
        else:
            x = mlp()
    return x.reshape(batch, seq, d)


def kernel(x_prompt, x_sample, norm_mix, a_w_in, a_conv_w, a_w_out, f_w_out,
           norm_ffn, w_up, w_down, final_norm):
    d = x_prompt.shape[-1]
    w = dict(
        norm_mix=norm_mix.reshape(-1, 1, d),
        norm_ffn=norm_ffn.reshape(-1, 1, d),
        final_norm=final_norm.reshape(1, 1, d),
        a_w_in=a_w_in.astype(BF16),
        a_conv_w=a_conv_w,
        a_w_out=a_w_out.astype(BF16),
        f_w_out=f_w_out.astype(BF16),
        w_up=w_up.astype(BF16),
        w_down=w_down.astype(BF16),
    )
    return _trunk(x_prompt, w), _trunk(x_sample, w)
```

```python
import functools
from typing import NamedTuple

import jax
import jax.numpy as jnp
import numpy as np
from jax import lax
from jax.experimental import pallas as pl
from jax.experimental.pallas import tpu as pltpu

F32 = jnp.float32
BF16 = jnp.bfloat16

RMS_EPS = 1e-6
N_FGROUPS = 8
LANES = 128
BF16_ROWS = 16
VMEM_LIMIT_BYTES = 60 * 1024 * 1024


def _params(*sem):
    return pltpu.CompilerParams(dimension_semantics=sem,
                                vmem_limit_bytes=VMEM_LIMIT_BYTES)


def _rms(x, g):
    ms = jnp.mean(x * x, axis=-1, keepdims=True)
    return (x * lax.rsqrt(ms + RMS_EPS)) * g


def _mlp_body(post, x_ref, g_ref, wu_ref, wd_ref, *rest):
    if post is None:
        acc_ref, h_ref = rest
    elif post == "also":
        g2_ref, acc_ref, n_ref, h_ref = rest
    else:
        g2_ref, n_ref, h_ref, acc_ref = rest
    j = pl.program_id(1)

    @pl.when(j == 0)
    def _():
        x = x_ref[...]
        h_ref[...] = _rms(x, g_ref[...]).astype(BF16)
        acc_ref[...] = x

    u = jnp.dot(h_ref[...], wu_ref[...], preferred_element_type=F32)
    u = jnp.maximum(u, 0.0)
    u = (u * u).astype(BF16)
    acc_ref[...] += jnp.dot(u, wd_ref[...], preferred_element_type=F32)

    if post is not None:
        @pl.when(j == pl.num_programs(1) - 1)
        def _():
            n_ref[...] = _rms(acc_ref[...], g2_ref[...]).astype(n_ref.dtype)


def _mlp(x, g, w_up, w_down, layer, post=None, g2=None, layer2=0, *, tm=512, tf=2048):
    t, d = x.shape
    ff = w_up.shape[-1]
    tile = pl.BlockSpec((tm, d), lambda i, j: (i, 0))
    in_specs = [
        tile,
        pl.BlockSpec((None, 1, d), lambda i, j: (layer, 0, 0)),
        pl.BlockSpec((None, d, tf), lambda i, j: (layer, 0, j)),
        pl.BlockSpec((None, tf, d), lambda i, j: (layer, j, 0)),
    ]
    args = [x, g, w_up, w_down]
    scratch = [pltpu.VMEM((tm, d), BF16)]
    x_new = jax.ShapeDtypeStruct((t, d), F32)
    if post is None:
        out_shape, out_specs = x_new, tile
    else:
        in_specs.append(pl.BlockSpec((None, 1, d), lambda i, j: (layer2, 0, 0)))
        args.append(g2)
        if post == "also":
            out_shape = (x_new, jax.ShapeDtypeStruct((t, d), BF16))
            out_specs = (tile, tile)
        else:
            out_shape, out_specs = x_new, tile
            scratch.append(pltpu.VMEM((tm, d), F32))
    return pl.pallas_call(
        functools.partial(_mlp_body, post),
        out_shape=out_shape,
        grid=(t // tm, ff // tf),
        in_specs=in_specs,
        out_specs=out_specs,
        scratch_shapes=scratch,
        compiler_params=_params("parallel", "arbitrary"),
        name="mlp",
    )(*args)


HALO = BF16_ROWS


def _conv_body(seq, tm, x_ref, xp_ref, xn_ref, g_ref, wb_ref, wc_ref, wv_ref,
               cw_ref, wo_ref, o_ref, h_ref):
    i = pl.program_id(0)

    @pl.when(pl.program_id(1) == 0)
    def _():
        g = g_ref[...]
        x = x_ref[...]
        keep_prev = ((i * tm) % seq != 0).astype(F32)
        keep_next = (((i + 1) * tm) % seq != 0).astype(F32)
        h_ref[0:HALO, :] = (_rms(xp_ref[...], g) * keep_prev).astype(BF16)
        h_ref[HALO:HALO + tm, :] = _rms(x, g).astype(BF16)
        h_ref[HALO + tm:, :] = (_rms(xn_ref[...], g) * keep_next).astype(BF16)
        o_ref[...] = x

    rows = tm + 2 * HALO
    h = h_ref[...]
    c = jnp.dot(h, wc_ref[...], preferred_element_type=F32)
    v = jnp.dot(h, wv_ref[...], preferred_element_type=F32)
    b = jnp.dot(h_ref[HALO:HALO + tm, :], wb_ref[...], preferred_element_type=F32)
    u = c * v
    cw = cw_ref[...]
    u_prev = pltpu.roll(u, 1, axis=0)[HALO:HALO + tm]
    u_next = pltpu.roll(u, rows - 1, axis=0)[HALO:HALO + tm]
    conv = cw[0:1] * u_prev + cw[1:2] * u[HALO:HALO + tm] + cw[2:3] * u_next
    gated = (b * conv).astype(BF16)
    o_ref[...] += jnp.dot(gated, wo_ref[...], preferred_element_type=F32)


def _conv_mixer(x, seq, g, w_in, conv_w, w_out, layer, *, tm=512, tc=1024):
    t, d = x.shape
    cdim = conv_w.shape[-1]
    nc = cdim // tc
    hb = tm // HALO
    last = t // HALO - 1
    return pl.pallas_call(
        functools.partial(_conv_body, seq, tm),
        out_shape=jax.ShapeDtypeStruct((t, d), F32),
        grid=(t // tm, nc),
        in_specs=[
            pl.BlockSpec((tm, d), lambda i, j: (i, 0)),
            pl.BlockSpec((HALO, d), lambda i, j: (jnp.maximum(i * hb - 1, 0), 0)),
            pl.BlockSpec((HALO, d), lambda i, j: (jnp.minimum((i + 1) * hb, last), 0)),
            pl.BlockSpec((None, 1, d), lambda i, j: (2 * layer, 0, 0)),
            pl.BlockSpec((None, d, tc), lambda i, j: (layer, 0, j)),
            pl.BlockSpec((None, d, tc), lambda i, j: (layer, 0, nc + j)),
            pl.BlockSpec((None, d, tc), lambda i, j: (layer, 0, 2 * nc + j)),
            pl.BlockSpec((None, 3, tc), lambda i, j: (layer, 0, j)),
            pl.BlockSpec((None, tc, d), lambda i, j: (layer, j, 0)),
        ],
        out_specs=pl.BlockSpec((tm, d), lambda i, j: (i, 0)),
        scratch_shapes=[pltpu.VMEM((tm + 2 * HALO, d), BF16)],
        compiler_params=_params("parallel", "arbitrary"),
        name="conv_mixer",
    )(x, x, x, g, w_in, w_in, w_in, conv_w, w_out)


def _dft_tables(seq, n1, n2, kb, cb, gdim, split_mid):
    c = np.arange(gdim)
    ang = 2.0 * np.pi * ((c[:, None] * c[None, :]) % gdim) / gdim
    fre, fim = np.cos(ang) / np.sqrt(gdim), -np.sin(ang) / np.sqrt(gdim)
    nh = gdim // cb
    fc = np.stack([np.concatenate([fre[:, q * cb:(q + 1) * cb],
                                   fim[:, q * cb:(q + 1) * cb]], axis=1)
                   for q in range(nh)])

    k = np.arange(n1)
    ang = 2.0 * np.pi * ((k[:, None] * k[None, :]) % n1) / n1
    gr, gi = np.cos(ang) / np.sqrt(n1), -np.sin(ang) / np.sqrt(n1)
    f1 = np.block([[gr, -gi], [gi, gr]])

    na = n1 // kb
    m = kb * n2
    h2 = np.zeros((na, m, 2 * m))
    nn = np.arange(n2)
    for a in range(na):
        for i in range(kb):
            k1 = a * kb + i
            kk = k1 + n1 * np.arange(n2)
            ang = 2.0 * np.pi * ((kk[:, None] * nn[None, :]) % seq) / seq
            wr, wi = np.cos(ang) / np.sqrt(n2), -np.sin(ang) / np.sqrt(n2)
            rows = np.arange(n2) * kb + i
            cols = nn * kb + i if split_mid else i * n2 + nn
            h2[a, rows[:, None], cols[None, :]] = wr
            h2[a, rows[:, None], m + cols[None, :]] = -wi
    return (np.asarray(fc, np.float32), np.asarray(f1, np.float32),
            np.asarray(h2, np.float32))


class FftCfg(NamedTuple):
    seq: int
    n1: int
    n2: int
    kb: int
    cb: int
    pitch_a: int
    pitch_o: int
    unroll: int
    h_buffers: int
    groups: int
    split_mid: bool


def _loop(n, body, unroll):
    if unroll >= n:
        for i in range(n):
            body(i)
    else:
        lax.fori_loop(0, n, lambda i, c: (body(i), c)[1], 0, unroll=unroll)


def _aligned(x, m):
    return x if isinstance(x, int) else pl.multiple_of(x, m)


def _fft_body(cfg, h_ref, fc_ref, f1_ref, h2_ref, o_ref, a_ref, *b_ref):
    seq, n1, n2, kb, cb, pa, po, unroll, _, groups, split_mid = cfg
    mid_ref = b_ref[0] if split_mid else a_ref
    nl = cb // LANES
    gdim = h_ref.shape[1] // groups
    rc = max(n2, min(seq, 512))
    per = rc // n2

    def chan(r):
        for g in range(groups):
            w = jnp.dot(h_ref[pl.ds(_aligned(r * rc, rc), rc), g * gdim:(g + 1) * gdim],
                        fc_ref[...], preferred_element_type=F32)
            for q in range(per if pa != n2 else 1):
                size = n2 if pa != n2 else rc
                dst = pl.ds(_aligned((r * per + q) * pa, 8), size)
                for p in range(2):
                    for s in range(nl):
                        lo = p * cb + s * LANES
                        a_ref[g, p, s, dst, :] = w[q * n2:q * n2 + size, lo:lo + LANES]
    _loop(seq // rc, chan, unroll)

    def load(ref, g, rows):
        return jnp.concatenate(
            [jnp.concatenate(
                [jnp.concatenate([ref[g, p, s, r, :] for r in rows], axis=0)
                 for s in range(nl)], axis=1)
             for p in range(2)], axis=0).astype(BF16)

    nb = max(1, 2 * LANES // cb)
    def stage1(t):
        rows = [pl.ds(t * nb + q, n1, stride=pa) for q in range(nb)]
        for g in range(groups):
            rhs = jnp.concatenate([load(a_ref, g, [r]) for r in rows], axis=1)
            y = jnp.dot(f1_ref[...], rhs, preferred_element_type=F32)
            for q in range(nb):
                dst = (pl.ds(_aligned((t * nb + q) * n1, n1), n1) if split_mid
                       else rows[q])
                for p in range(2):
                    for s in range(nl):
                        lo = q * cb + s * LANES
                        mid_ref[g, p, s, dst, :] = y[p * n1:(p + 1) * n1, lo:lo + LANES]
    _loop(n2 // nb, stage1, unroll)

    m = kb * n2
    def stage2(a):
        for g in range(groups):
            if split_mid:
                src = [pl.ds(i2 * n1 + _aligned(a * kb, kb), kb) for i2 in range(n2)]
            else:
                src = [pl.ds(_aligned(a * kb * pa, 8), m)]
            y = jnp.dot(h2_ref[a], load(mid_ref, g, src),
                        preferred_element_type=F32)
            cols = slice(g * cb, (g + 1) * cb)
            if kb == 1:
                o_ref[pl.ds(a, n2, stride=po), cols] = y
            else:
                for k2 in range(n2):
                    o_ref[pl.ds(_aligned(a * kb, kb) + k2 * po, kb), cols] = (
                        y[k2 * kb:(k2 + 1) * kb])
    _loop(n1 // kb, stage2, unroll)

    if po != n1:
        for k2 in range(n2):
            o_ref[k2 * po + n1:(k2 + 1) * po, :] = jnp.zeros(
                (po - n1, groups * cb), F32)


def _fourier(h, batch, cfg):
    t, d = h.shape
    seq, n1, n2, kb, cb, pa, po, _, _, groups, split_mid = cfg
    gdim = d // N_FGROUPS
    nh = gdim // cb
    assert kb == 1 or pa == n2 or split_mid
    assert groups == 1 or nh == 1
    assert not split_mid or kb % 8 == 0
    fc, f1, h2 = (jnp.asarray(tab).astype(BF16)
                  for tab in _dft_tables(seq, n1, n2, kb, cb, gdim, split_mid))
    out = pl.pallas_call(
        functools.partial(_fft_body, cfg),
        out_shape=jax.ShapeDtypeStruct((batch, n2 * po, d), F32),
        grid=(batch, d // (cb * groups)),
        in_specs=[
            pl.BlockSpec((None, seq, gdim * groups), lambda b, l: (b, 0, l // nh),
                         pipeline_mode=pl.Buffered(cfg.h_buffers)),
            pl.BlockSpec((None, gdim, 2 * cb), lambda b, l: (l % nh, 0, 0)),
            pl.BlockSpec(f1.shape, lambda b, l: (0, 0), pipeline_mode=pl.Buffered(1)),
            pl.BlockSpec(h2.shape, lambda b, l: (0, 0, 0), pipeline_mode=pl.Buffered(1)),
        ],
        out_specs=pl.BlockSpec((None, n2 * po, cb * groups), lambda b, l: (b, 0, l)),
        scratch_shapes=[pltpu.VMEM((groups, 2, cb // LANES, n1 * pa, LANES), F32)]
        + [pltpu.VMEM((groups, 2, cb // LANES, seq, LANES), F32)] * split_mid,
        compiler_params=_params("parallel", "arbitrary"),
        name="fourier",
    )(h.reshape(batch, seq, d), fc, f1, h2)
    return out.reshape(batch * n2 * po, d)


def _proj_body(n1, po, x_ref, m_ref, w_ref, o_ref):
    if po == n1:
        m = m_ref[...]
    else:
        m = jnp.concatenate([m_ref[q * po:q * po + n1, :]
                             for q in range(m_ref.shape[0] // po)], axis=0)
    o_ref[...] = x_ref[...] + jnp.dot(m.astype(BF16), w_ref[...],
                                      preferred_element_type=F32)


def _proj_residual(x, mixed, cfg, w, layer, *, tm=512):
    t, d = x.shape
    mrows = tm // cfg.n1 * cfg.pitch_o
    return pl.pallas_call(
        functools.partial(_proj_body, cfg.n1, cfg.pitch_o),
        out_shape=jax.ShapeDtypeStruct((t, d), F32),
        grid=(t // tm,),
        in_specs=[pl.BlockSpec((tm, d), lambda i: (i, 0)),
                  pl.BlockSpec((mrows, d), lambda i: (i, 0)),
                  pl.BlockSpec((None, d, d), lambda i: (layer, 0, 0),
                               pipeline_mode=pl.Buffered(1))],
        out_specs=pl.BlockSpec((tm, d), lambda i: (i, 0)),
        compiler_params=_params("parallel"),
        name="proj_residual",
    )(x, mixed, w)


_FFT_CONFIG = {
    16384: FftCfg(seq=16384, n1=128, n2=128, kb=1, cb=128, pitch_a=136,
                  pitch_o=136, unroll=4, h_buffers=1, groups=1, split_mid=False),
    2048: FftCfg(seq=2048, n1=128, n2=16, kb=8, cb=256, pitch_a=24,
                 pitch_o=128, unroll=16, h_buffers=2, groups=2, split_mid=True),
}


def _trunk(x3, w):
    batch, seq, d = x3.shape
    x = x3.reshape(batch * seq, d)
    depth = w["norm_ffn"].shape[0]
    h = None
    for i in range(depth):
        j = i // 2
        if i % 2 == 0:
            x = _conv_mixer(x, seq, w["norm_mix"], w["a_w_in"], w["a_conv_w"],
                            w["a_w_out"], j)
        else:
            cfg = _FFT_CONFIG[seq]
            mixed = _fourier(h, batch, cfg)
            x = _proj_residual(x, mixed, cfg, w["f_w_out"], j)
        mlp = functools.partial(_mlp, x, w["norm_ffn"], w["w_up"], w["w_down"], i)
        if i == depth - 1:
            x = mlp("only", w["final_norm"], 0)
        elif (i + 1) % 2 == 1:
            x, h = mlp("also", w["norm_mix"], i + 1)
        else:
            x = mlp()
    return x.reshape(batch, seq, d)


def kernel(x_prompt, x_sample, norm_mix, a_w_in, a_conv_w, a_w_out, f_w_out,
           norm_ffn, w_up, w_down, final_norm):
    d = x_prompt.shape[-1]
    w = dict(
        norm_mix=norm_mix.reshape(-1, 1, d),
        norm_ffn=norm_ffn.reshape(-1, 1, d),
        final_norm=final_norm.reshape(1, 1, d),
        a_w_in=a_w_in.astype(BF16),
        a_conv_w=a_conv_w,
        a_w_out=a_w_out.astype(BF16),
        f_w_out=f_w_out.astype(BF16),
        w_up=w_up.astype(BF16),
        w_down=w_down.astype(BF16),
    )
    return _trunk(x_prompt, w), _trunk(x_sample, w)
```

```python
import functools
from typing import NamedTuple

import jax
import jax.numpy as jnp
import numpy as np
from jax import lax
from jax.experimental import pallas as pl
from jax.experimental.pallas import tpu as pltpu

F32 = jnp.float32
BF16 = jnp.bfloat16

RMS_EPS = 1e-6
N_FGROUPS = 8
LANES = 128
BF16_ROWS = 16
VMEM_LIMIT_BYTES = 60 * 1024 * 1024


def _params(*sem):
    return pltpu.CompilerParams(dimension_semantics=sem,
                                vmem_limit_bytes=VMEM_LIMIT_BYTES)


def _rms(x, g):
    ms = jnp.mean(x * x, axis=-1, keepdims=True)
    return (x * lax.rsqrt(ms + RMS_EPS)) * g


def _mlp_body(post, x_ref, g_ref, wu_ref, wd_ref, *rest):
    if post is None:
        acc_ref, h_ref = rest
    elif post == "also":
        g2_ref, acc_ref, n_ref, h_ref = rest
    else:
        g2_ref, n_ref, h_ref, acc_ref = rest
    j = pl.program_id(1)

    @pl.when(j == 0)
    def _():
        x = x_ref[...]
        h_ref[...] = _rms(x, g_ref[...]).astype(BF16)
        acc_ref[...] = x

    u = jnp.dot(h_ref[...], wu_ref[...], preferred_element_type=F32)
    u = jnp.maximum(u, 0.0)
    u = (u * u).astype(BF16)
    acc_ref[...] += jnp.dot(u, wd_ref[...], preferred_element_type=F32)

    if post is not None:
        @pl.when(j == pl.num_programs(1) - 1)
        def _():
            n_ref[...] = _rms(acc_ref[...], g2_ref[...]).astype(n_ref.dtype)


def _mlp(x, g, w_up, w_down, layer, post=None, g2=None, layer2=0, *, tm=512, tf=2048):
    t, d = x.shape
    ff = w_up.shape[-1]
    tile = pl.BlockSpec((tm, d), lambda i, j: (i, 0))
    in_specs = [
        tile,
        pl.BlockSpec((None, 1, d), lambda i, j: (layer, 0, 0)),
        pl.BlockSpec((None, d, tf), lambda i, j: (layer, 0, j)),
        pl.BlockSpec((None, tf, d), lambda i, j: (layer, j, 0)),
    ]
    args = [x, g, w_up, w_down]
    scratch = [pltpu.VMEM((tm, d), BF16)]
    x_new = jax.ShapeDtypeStruct((t, d), F32)
    if post is None:
        out_shape, out_specs = x_new, tile
    else:
        in_specs.append(pl.BlockSpec((None, 1, d), lambda i, j: (layer2, 0, 0)))
        args.append(g2)
        if post == "also":
            out_shape = (x_new, jax.ShapeDtypeStruct((t, d), BF16))
            out_specs = (tile, tile)
        else:
            out_shape, out_specs = x_new, tile
            scratch.append(pltpu.VMEM((tm, d), F32))
    return pl.pallas_call(
        functools.partial(_mlp_body, post),
        out_shape=out_shape,
        grid=(t // tm, ff // tf),
        in_specs=in_specs,
        out_specs=out_specs,
        scratch_shapes=scratch,
        compiler_params=_params("parallel", "arbitrary"),
        name="mlp",
    )(*args)


HALO = BF16_ROWS


def _conv_body(seq, tm, x_ref, xp_ref, xn_ref, g_ref, wb_ref, wc_ref, wv_ref,
               cw_ref, wo_ref, o_ref, h_ref):
    i = pl.program_id(0)

    @pl.when(pl.program_id(1) == 0)
    def _():
        g = g_ref[...]
        x = x_ref[...]
        keep_prev = ((i * tm) % seq != 0).astype(F32)
        keep_next = (((i + 1) * tm) % seq != 0).astype(F32)
        h_ref[0:HALO, :] = (_rms(xp_ref[...], g) * keep_prev).astype(BF16)
        h_ref[HALO:HALO + tm, :] = _rms(x, g).astype(BF16)
        h_ref[HALO + tm:, :] = (_rms(xn_ref[...], g) * keep_next).astype(BF16)
        o_ref[...] = x

    rows = tm + 2 * HALO
    h = h_ref[...]
    c = jnp.dot(h, wc_ref[...], preferred_element_type=F32)
    v = jnp.dot(h, wv_ref[...], preferred_element_type=F32)
    b = jnp.dot(h_ref[HALO:HALO + tm, :], wb_ref[...], preferred_element_type=F32)
    u = c * v
    cw = cw_ref[...]
    u_prev = pltpu.roll(u, 1, axis=0)[HALO:HALO + tm]
    u_next = pltpu.roll(u, rows - 1, axis=0)[HALO:HALO + tm]
    conv = cw[0:1] * u_prev + cw[1:2] * u[HALO:HALO + tm] + cw[2:3] * u_next
    gated = (b * conv).astype(BF16)
    o_ref[...] += jnp.dot(gated, wo_ref[...], preferred_element_type=F32)


def _conv_mixer(x, seq, g, w_in, conv_w, w_out, layer, *, tm=512, tc=1024):
    t, d = x.shape
    cdim = conv_w.shape[-1]
    nc = cdim // tc
    hb = tm // HALO
    last = t // HALO - 1
    return pl.pallas_call(
        functools.partial(_conv_body, seq, tm),
        out_shape=jax.ShapeDtypeStruct((t, d), F32),
        grid=(t // tm, nc),
        in_specs=[
            pl.BlockSpec((tm, d), lambda i, j: (i, 0)),
            pl.BlockSpec((HALO, d), lambda i, j: (jnp.maximum(i * hb - 1, 0), 0)),
            pl.BlockSpec((HALO, d), lambda i, j: (jnp.minimum((i + 1) * hb, last), 0)),
            pl.BlockSpec((None, 1, d), lambda i, j: (2 * layer, 0, 0)),
            pl.BlockSpec((None, d, tc), lambda i, j: (layer, 0, j)),
            pl.BlockSpec((None, d, tc), lambda i, j: (layer, 0, nc + j)),
            pl.BlockSpec((None, d, tc), lambda i, j: (layer, 0, 2 * nc + j)),
            pl.BlockSpec((None, 3, tc), lambda i, j: (layer, 0, j)),
            pl.BlockSpec((None, tc, d), lambda i, j: (layer, j, 0)),
        ],
        out_specs=pl.BlockSpec((tm, d), lambda i, j: (i, 0)),
        scratch_shapes=[pltpu.VMEM((tm + 2 * HALO, d), BF16)],
        compiler_params=_params("parallel", "arbitrary"),
        name="conv_mixer",
    )(x, x, x, g, w_in, w_in, w_in, conv_w, w_out)


def _dft_tables(seq, n1, n2, kb, cb, gdim, split_mid):
    c = np.arange(gdim)
    ang = 2.0 * np.pi * ((c[:, None] * c[None, :]) % gdim) / gdim
    fre, fim = np.cos(ang) / np.sqrt(gdim), -np.sin(ang) / np.sqrt(gdim)
    nh = gdim // cb
    fc = np.stack([np.concatenate([fre[:, q * cb:(q + 1) * cb],
                                   fim[:, q * cb:(q + 1) * cb]], axis=1)
                   for q in range(nh)])

    k = np.arange(n1)
    ang = 2.0 * np.pi * ((k[:, None] * k[None, :]) % n1) / n1
    gr, gi = np.cos(ang) / np.sqrt(n1), -np.sin(ang) / np.sqrt(n1)
    f1 = np.block([[gr, -gi], [gi, gr]])

    na = n1 // kb
    m = kb * n2
    h2 = np.zeros((na, m, 2 * m))
    nn = np.arange(n2)
    for a in range(na):
        for i in range(kb):
            k1 = a * kb + i
            kk = k1 + n1 * np.arange(n2)
            ang = 2.0 * np.pi * ((kk[:, None] * nn[None, :]) % seq) / seq
            wr, wi = np.cos(ang) / np.sqrt(n2), -np.sin(ang) / np.sqrt(n2)
            rows = np.arange(n2) * kb + i
            cols = nn * kb + i if split_mid else i * n2 + nn
            h2[a, rows[:, None], cols[None, :]] = wr
            h2[a, rows[:, None], m + cols[None, :]] = -wi
    return (np.asarray(fc, np.float32), np.asarray(f1, np.float32),
            np.asarray(h2, np.float32))


class FftCfg(NamedTuple):
    seq: int
    n1: int
    n2: int
    kb: int
    cb: int
    pitch_a: int
    pitch_o: int
    unroll: int
    h_buffers: int
    groups: int
    split_mid: bool


def _loop(n, body, unroll):
    if unroll >= n:
        for i in range(n):
            body(i)
    else:
        lax.fori_loop(0, n, lambda i, c: (body(i), c)[1], 0, unroll=unroll)


def _aligned(x, m):
    return x if isinstance(x, int) else pl.multiple_of(x, m)


def _fft_body(cfg, h_ref, fc_ref, f1_ref, h2_ref, o_ref, a_ref, *b_ref):
    seq, n1, n2, kb, cb, pa, po, unroll, _, groups, split_mid = cfg
    mid_ref = b_ref[0] if split_mid else a_ref
    nl = cb // LANES
    gdim = h_ref.shape[1] // groups
    rc = max(n2, min(seq, 512))
    per = rc // n2

    def chan(r):
        for g in range(groups):
            w = jnp.dot(h_ref[pl.ds(_aligned(r * rc, rc), rc), g * gdim:(g + 1) * gdim],
                        fc_ref[...], preferred_element_type=F32)
            for q in range(per if pa != n2 else 1):
                size = n2 if pa != n2 else rc
                dst = pl.ds(_aligned((r * per + q) * pa, 8), size)
                for p in range(2):
                    for s in range(nl):
                        lo = p * cb + s * LANES
                        a_ref[g, p, s, dst, :] = w[q * n2:q * n2 + size, lo:lo + LANES]
    _loop(seq // rc, chan, unroll)

    def load(ref, g, rows):
        return jnp.concatenate(
            [jnp.concatenate(
                [jnp.concatenate([ref[g, p, s, r, :] for r in rows], axis=0)
                 for s in range(nl)], axis=1)
             for p in range(2)], axis=0).astype(BF16)

    nb = max(1, 2 * LANES // cb)
    def stage1(t):
        rows = [pl.ds(t * nb + q, n1, stride=pa) for q in range(nb)]
        for g in range(groups):
            rhs = jnp.concatenate([load(a_ref, g, [r]) for r in rows], axis=1)
            y = jnp.dot(f1_ref[...], rhs, preferred_element_type=F32)
            for q in range(nb):
                dst = (pl.ds(_aligned((t * nb + q) * n1, n1), n1) if split_mid
                       else rows[q])
                for p in range(2):
                    for s in range(nl):
                        lo = q * cb + s * LANES
                        mid_ref[g, p, s, dst, :] = y[p * n1:(p + 1) * n1, lo:lo + LANES]
    _loop(n2 // nb, stage1, unroll)

    m = kb * n2
    def stage2(a):
        for g in range(groups):
            if split_mid:
                src = [pl.ds(i2 * n1 + _aligned(a * kb, kb), kb) for i2 in range(n2)]
            else:
                src = [pl.ds(_aligned(a * kb * pa, 8), m)]
            y = jnp.dot(h2_ref[a], load(mid_ref, g, src),
                        preferred_element_type=F32)
            cols = slice(g * cb, (g + 1) * cb)
            if kb == 1:
                o_ref[pl.ds(a, n2, stride=po), cols] = y
            else:
                for k2 in range(n2):
                    o_ref[pl.ds(_aligned(a * kb, kb) + k2 * po, kb), cols] = (
                        y[k2 * kb:(k2 + 1) * kb])
    _loop(n1 // kb, stage2, unroll)

    if po != n1:
        for k2 in range(n2):
            o_ref[k2 * po + n1:(k2 + 1) * po, :] = jnp.zeros(
                (po - n1, groups * cb), F32)


def _fourier(h, batch, cfg):
    t, d = h.shape
    seq, n1, n2, kb, cb, pa, po, _, _, groups, split_mid = cfg
    gdim = d // N_FGROUPS
    nh = gdim // cb
    assert kb == 1 or pa == n2 or split_mid
    assert groups == 1 or nh == 1
    assert not split_mid or kb % 8 == 0
    fc, f1, h2 = (jnp.asarray(tab).astype(BF16)
                  for tab in _dft_tables(seq, n1, n2, kb, cb, gdim, split_mid))
    out = pl.pallas_call(
        functools.partial(_fft_body, cfg),
        out_shape=jax.ShapeDtypeStruct((batch, n2 * po, d), F32),
        grid=(batch, d // (cb * groups)),
        in_specs=[
            pl.BlockSpec((None, seq, gdim * groups), lambda b, l: (b, 0, l // nh),
                         pipeline_mode=pl.Buffered(cfg.h_buffers)),
            pl.BlockSpec((None, gdim, 2 * cb), lambda b, l: (l % nh, 0, 0)),
            pl.BlockSpec(f1.shape, lambda b, l: (0, 0), pipeline_mode=pl.Buffered(1)),
            pl.BlockSpec(h2.shape, lambda b, l: (0, 0, 0), pipeline_mode=pl.Buffered(1)),
        ],
        out_specs=pl.BlockSpec((None, n2 * po, cb * groups), lambda b, l: (b, 0, l)),
        scratch_shapes=[pltpu.VMEM((groups, 2, cb // LANES, n1 * pa, LANES), F32)]
        + [pltpu.VMEM((groups, 2, cb // LANES, seq, LANES), F32)] * split_mid,
        compiler_params=_params("parallel", "arbitrary"),
        name="fourier",
    )(h.reshape(batch, seq, d), fc, f1, h2)
    return out.reshape(batch * n2 * po, d)


def _proj_body(n1, po, x_ref, m_ref, w_ref, o_ref):
    if po == n1:
        m = m_ref[...]
    else:
        m = jnp.concatenate([m_ref[q * po:q * po + n1, :]
                             for q in range(m_ref.shape[0] // po)], axis=0)
    o_ref[...] = x_ref[...] + jnp.dot(m.astype(BF16), w_ref[...],
                                      preferred_element_type=F32)


def _proj_residual(x, mixed, cfg, w, layer, *, tm=512):
    t, d = x.shape
    mrows = tm // cfg.n1 * cfg.pitch_o
    return pl.pallas_call(
        functools.partial(_proj_body, cfg.n1, cfg.pitch_o),
        out_shape=jax.ShapeDtypeStruct((t, d), F32),
        grid=(t // tm,),
        in_specs=[pl.BlockSpec((tm, d), lambda i: (i, 0)),
                  pl.BlockSpec((mrows, d), lambda i: (i, 0)),
                  pl.BlockSpec((None, d, d), lambda i: (layer, 0, 0),
                               pipeline_mode=pl.Buffered(1))],
        out_specs=pl.BlockSpec((tm, d), lambda i: (i, 0)),
        compiler_params=_params("parallel"),
        name="proj_residual",
    )(x, mixed, w)


_FFT_CONFIG = {
    16384: FftCfg(seq=16384, n1=128, n2=128, kb=1, cb=128, pitch_a=136,
                  pitch_o=136, unroll=16, h_buffers=1, groups=1, split_mid=False),
    2048: FftCfg(seq=2048, n1=128, n2=16, kb=8, cb=256, pitch_a=24,
                 pitch_o=128, unroll=16, h_buffers=2, groups=2, split_mid=True),
}


def _trunk(x3, w):
    batch, seq, d = x3.shape
    x = x3.reshape(batch * seq, d)
    depth = w["norm_ffn"].shape[0]
    h = None
    for i in range(depth):
        j = i // 2
        if i % 2 == 0:
            x = _conv_mixer(x, seq, w["norm_mix"], w["a_w_in"], w["a_conv_w"],
                            w["a_w_out"], j)
        else:
            cfg = _FFT_CONFIG[seq]
            mixed = _fourier(h, batch, cfg)
            x = _proj_residual(x, mixed, cfg, w["f_w_out"], j)
        mlp = functools.partial(_mlp, x, w["norm_ffn"], w["w_up"], w["w_down"], i)
        if i == depth - 1:
            x = mlp("only", w["final_norm"], 0)
        elif (i + 1) % 2 == 1:
            x, h = mlp("also", w["norm_mix"], i + 1)
        else:
            x = mlp()
    return x.reshape(batch, seq, d)


def kernel(x_prompt, x_sample, norm_mix, a_w_in, a_conv_w, a_w_out, f_w_out,
           norm_ffn, w_up, w_down, final_norm):
    d = x_prompt.shape[-1]
    w = dict(
        norm_mix=norm_mix.reshape(-1, 1, d),
        norm_ffn=norm_ffn.reshape(-1, 1, d),
        final_norm=final_norm.reshape(1, 1, d),
        a_w_in=a_w_in.astype(BF16),
        a_conv_w=a_conv_w,
        a_w_out=a_w_out.astype(BF16),
        f_w_out=f_w_out.astype(BF16),
        w_up=w_up.astype(BF16),
        w_down=w_down.astype(BF16),
    )
    return _trunk(x_prompt, w), _trunk(x_sample, w)
```

```python
import functools
from typing import NamedTuple

import jax
import jax.numpy as jnp
import numpy as np
from jax import lax
from jax.experimental import pallas as pl
from jax.experimental.pallas import tpu as pltpu

F32 = jnp.float32
BF16 = jnp.bfloat16

RMS_EPS = 1e-6
N_FGROUPS = 8
LANES = 128
BF16_ROWS = 16
VMEM_LIMIT_BYTES = 60 * 1024 * 1024


def _params(*sem):
    return pltpu.CompilerParams(dimension_semantics=sem,
                                vmem_limit_bytes=VMEM_LIMIT_BYTES)


def _rms(x, g):
    ms = jnp.mean(x * x, axis=-1, keepdims=True)
    return (x * lax.rsqrt(ms + RMS_EPS)) * g


def _mlp_body(post, x_ref, g_ref, wu_ref, wd_ref, *rest):
    if post is None:
        acc_ref, h_ref = rest
    elif post == "also":
        g2_ref, acc_ref, n_ref, h_ref = rest
    else:
        g2_ref, n_ref, h_ref, acc_ref = rest
    j = pl.program_id(1)
    last_j = pl.num_programs(1) - 1

    def step(first, last):
        if first:
            x = x_ref[...]
            h = _rms(x, g_ref[...]).astype(BF16)
            h_ref[...] = h
        else:
            h = h_ref[...]
        u = jnp.dot(h, wu_ref[...], preferred_element_type=F32)
        u = jnp.maximum(u, 0.0)
        u = (u * u).astype(BF16)
        y = (x_ref[...] if first else acc_ref[...]) + jnp.dot(
            u, wd_ref[...], preferred_element_type=F32)
        if not (last and post == "only"):
            acc_ref[...] = y
        if last and post is not None:
            n_ref[...] = _rms(y, g2_ref[...]).astype(n_ref.dtype)

    pl.when(j == 0)(functools.partial(step, True, False))
    if post is None:
        pl.when(j > 0)(functools.partial(step, False, False))
    else:
        pl.when((j > 0) & (j < last_j))(functools.partial(step, False, False))
        pl.when(j == last_j)(functools.partial(step, False, True))


def _mlp(x, g, w_up, w_down, layer, post=None, g2=None, layer2=0, *, tm=512, tf=2048):
    t, d = x.shape
    ff = w_up.shape[-1]
    assert ff // tf >= 2
    tile = pl.BlockSpec((tm, d), lambda i, j: (i, 0))
    in_specs = [
        tile,
        pl.BlockSpec((None, 1, d), lambda i, j: (layer, 0, 0)),
        pl.BlockSpec((None, d, tf), lambda i, j: (layer, 0, j)),
        pl.BlockSpec((None, tf, d), lambda i, j: (layer, j, 0)),
    ]
    args = [x, g, w_up, w_down]
    scratch = [pltpu.VMEM((tm, d), BF16)]
    x_new = jax.ShapeDtypeStruct((t, d), F32)
    if post is None:
        out_shape, out_specs = x_new, tile
    else:
        in_specs.append(pl.BlockSpec((None, 1, d), lambda i, j: (layer2, 0, 0)))
        args.append(g2)
        if post == "also":
            out_shape = (x_new, jax.ShapeDtypeStruct((t, d), BF16))
            out_specs = (tile, tile)
        else:
            out_shape, out_specs = x_new, tile
            scratch.append(pltpu.VMEM((tm, d), F32))
    return pl.pallas_call(
        functools.partial(_mlp_body, post),
        out_shape=out_shape,
        grid=(t // tm, ff // tf),
        in_specs=in_specs,
        out_specs=out_specs,
        scratch_shapes=scratch,
        compiler_params=_params("parallel", "arbitrary"),
        name="mlp",
    )(*args)


HALO = BF16_ROWS


def _conv_body(seq, tm, x_ref, xp_ref, xn_ref, g_ref, wb_ref, wc_ref, wv_ref,
               cw_ref, wo_ref, o_ref, h_ref):
    i = pl.program_id(0)
    j = pl.program_id(1)
    rows = tm + 2 * HALO

    def step(first):
        if first:
            g = g_ref[...]
            keep_prev = ((i * tm) % seq != 0).astype(F32)
            keep_next = (((i + 1) * tm) % seq != 0).astype(F32)
            h = jnp.concatenate(
                [(_rms(xp_ref[...], g) * keep_prev).astype(BF16),
                 _rms(x_ref[...], g).astype(BF16),
                 (_rms(xn_ref[...], g) * keep_next).astype(BF16)], axis=0)
            h_ref[...] = h
        else:
            h = h_ref[...]
        c = jnp.dot(h, wc_ref[...], preferred_element_type=F32)
        v = jnp.dot(h, wv_ref[...], preferred_element_type=F32)
        b = jnp.dot(h[HALO:HALO + tm], wb_ref[...], preferred_element_type=F32)
        u = c * v
        cw = cw_ref[...]
        u_prev = pltpu.roll(u, 1, axis=0)[HALO:HALO + tm]
        u_next = pltpu.roll(u, rows - 1, axis=0)[HALO:HALO + tm]
        conv = cw[0:1] * u_prev + cw[1:2] * u[HALO:HALO + tm] + cw[2:3] * u_next
        gated = (b * conv).astype(BF16)
        o_ref[...] = (x_ref[...] if first else o_ref[...]) + jnp.dot(
            gated, wo_ref[...], preferred_element_type=F32)

    pl.when(j == 0)(functools.partial(step, True))
    pl.when(j > 0)(functools.partial(step, False))


def _conv_mixer(x, seq, g, w_in, conv_w, w_out, layer, *, tm=512, tc=1024):
    t, d = x.shape
    cdim = conv_w.shape[-1]
    nc = cdim // tc
    hb = tm // HALO
    last = t // HALO - 1
    return pl.pallas_call(
        functools.partial(_conv_body, seq, tm),
        out_shape=jax.ShapeDtypeStruct((t, d), F32),
        grid=(t // tm, nc),
        in_specs=[
            pl.BlockSpec((tm, d), lambda i, j: (i, 0)),
            pl.BlockSpec((HALO, d), lambda i, j: (jnp.maximum(i * hb - 1, 0), 0)),
            pl.BlockSpec((HALO, d), lambda i, j: (jnp.minimum((i + 1) * hb, last), 0)),
            pl.BlockSpec((None, 1, d), lambda i, j: (2 * layer, 0, 0)),
            pl.BlockSpec((None, d, tc), lambda i, j: (layer, 0, j)),
            pl.BlockSpec((None, d, tc), lambda i, j: (layer, 0, nc + j)),
            pl.BlockSpec((None, d, tc), lambda i, j: (layer, 0, 2 * nc + j)),
            pl.BlockSpec((None, 3, tc), lambda i, j: (layer, 0, j)),
            pl.BlockSpec((None, tc, d), lambda i, j: (layer, j, 0)),
        ],
        out_specs=pl.BlockSpec((tm, d), lambda i, j: (i, 0)),
        scratch_shapes=[pltpu.VMEM((tm + 2 * HALO, d), BF16)],
        compiler_params=_params("parallel", "arbitrary"),
        name="conv_mixer",
    )(x, x, x, g, w_in, w_in, w_in, conv_w, w_out)


def _dft_tables(seq, n1, n2, kb, cb, gdim, split_mid):
    c = np.arange(gdim)
    ang = 2.0 * np.pi * ((c[:, None] * c[None, :]) % gdim) / gdim
    fre, fim = np.cos(ang) / np.sqrt(gdim), -np.sin(ang) / np.sqrt(gdim)
    nh = gdim // cb
    fc = np.stack([np.concatenate([fre[:, q * cb:(q + 1) * cb],
                                   fim[:, q * cb:(q + 1) * cb]], axis=1)
                   for q in range(nh)])

    k = np.arange(n1)
    ang = 2.0 * np.pi * ((k[:, None] * k[None, :]) % n1) / n1
    gr, gi = np.cos(ang) / np.sqrt(n1), -np.sin(ang) / np.sqrt(n1)
    f1 = np.block([[gr, -gi], [gi, gr]])

    na = n1 // kb
    m = kb * n2
    h2 = np.zeros((na, m, 2 * m))
    nn = np.arange(n2)
    for a in range(na):
        for i in range(kb):
            k1 = a * kb + i
            kk = k1 + n1 * np.arange(n2)
            ang = 2.0 * np.pi * ((kk[:, None] * nn[None, :]) % seq) / seq
            wr, wi = np.cos(ang) / np.sqrt(n2), -np.sin(ang) / np.sqrt(n2)
            rows = np.arange(n2) * kb + i
            cols = nn * kb + i if split_mid else i * n2 + nn
            h2[a, rows[:, None], cols[None, :]] = wr
            h2[a, rows[:, None], m + cols[None, :]] = -wi
    return (np.asarray(fc, np.float32), np.asarray(f1, np.float32),
            np.asarray(h2, np.float32))


class FftCfg(NamedTuple):
    seq: int
    n1: int
    n2: int
    kb: int
    cb: int
    pitch_a: int
    pitch_o: int
    unroll: int
    h_buffers: int
    groups: int
    split_mid: bool


def _loop(n, body, unroll):
    if unroll >= n:
        for i in range(n):
            body(i)
    else:
        lax.fori_loop(0, n, lambda i, c: (body(i), c)[1], 0, unroll=unroll)


def _aligned(x, m):
    return x if isinstance(x, int) else pl.multiple_of(x, m)


def _fft_body(cfg, h_ref, fc_ref, f1_ref, h2_ref, o_ref, a_ref, *b_ref):
    seq, n1, n2, kb, cb, pa, po, unroll, _, groups, split_mid = cfg
    mid_ref = b_ref[0] if split_mid else a_ref
    nl = cb // LANES
    gdim = h_ref.shape[1] // groups
    rc = max(n2, min(seq, 512))
    per = rc // n2

    def chan(r):
        for g in range(groups):
            w = jnp.dot(h_ref[pl.ds(_aligned(r * rc, rc), rc), g * gdim:(g + 1) * gdim],
                        fc_ref[...], preferred_element_type=F32)
            for q in range(per if pa != n2 else 1):
                size = n2 if pa != n2 else rc
                dst = pl.ds(_aligned((r * per + q) * pa, 8), size)
                for p in range(2):
                    for s in range(nl):
                        lo = p * cb + s * LANES
                        a_ref[g, p, s, dst, :] = w[q * n2:q * n2 + size, lo:lo + LANES]
    _loop(seq // rc, chan, unroll)

    def load(ref, g, rows):
        return jnp.concatenate(
            [jnp.concatenate(
                [jnp.concatenate([ref[g, p, s, r, :] for r in rows], axis=0)
                 for s in range(nl)], axis=1)
             for p in range(2)], axis=0).astype(BF16)

    nb = max(1, 2 * LANES // cb)
    def stage1(t):
        rows = [pl.ds(t * nb + q, n1, stride=pa) for q in range(nb)]
        for g in range(groups):
            rhs = jnp.concatenate([load(a_ref, g, [r]) for r in rows], axis=1)
            y = jnp.dot(f1_ref[...], rhs, preferred_element_type=F32)
            for q in range(nb):
                dst = (pl.ds(_aligned((t * nb + q) * n1, n1), n1) if split_mid
                       else rows[q])
                for p in range(2):
                    for s in range(nl):
                        lo = q * cb + s * LANES
                        mid_ref[g, p, s, dst, :] = y[p * n1:(p + 1) * n1, lo:lo + LANES]
    _loop(n2 // nb, stage1, unroll)

    m = kb * n2
    def stage2(a):
        for g in range(groups):
            if split_mid:
                src = [pl.ds(i2 * n1 + _aligned(a * kb, kb), kb) for i2 in range(n2)]
            else:
                src = [pl.ds(_aligned(a * kb * pa, 8), m)]
            y = jnp.dot(h2_ref[a], load(mid_ref, g, src),
                        preferred_element_type=F32)
            cols = slice(g * cb, (g + 1) * cb)
            if kb == 1:
                o_ref[pl.ds(a, n2, stride=po), cols] = y
            else:
                for k2 in range(n2):
                    o_ref[pl.ds(_aligned(a * kb, kb) + k2 * po, kb), cols] = (
                        y[k2 * kb:(k2 + 1) * kb])
    _loop(n1 // kb, stage2, unroll)

    if po != n1:
        for k2 in range(n2):
            o_ref[k2 * po + n1:(k2 + 1) * po, :] = jnp.zeros(
                (po - n1, groups * cb), F32)


def _fourier(h, batch, cfg):
    t, d = h.shape
    seq, n1, n2, kb, cb, pa, po, _, _, groups, split_mid = cfg
    gdim = d // N_FGROUPS
    nh = gdim // cb
    assert kb == 1 or pa == n2 or split_mid
    assert groups == 1 or nh == 1
    assert not split_mid or kb % 8 == 0
    fc, f1, h2 = (jnp.asarray(tab).astype(BF16)
                  for tab in _dft_tables(seq, n1, n2, kb, cb, gdim, split_mid))
    out = pl.pallas_call(
        functools.partial(_fft_body, cfg),
        out_shape=jax.ShapeDtypeStruct((batch, n2 * po, d), F32),
        grid=(batch, d // (cb * groups)),
        in_specs=[
            pl.BlockSpec((None, seq, gdim * groups), lambda b, l: (b, 0, l // nh),
                         pipeline_mode=pl.Buffered(cfg.h_buffers)),
            pl.BlockSpec((None, gdim, 2 * cb), lambda b, l: (l % nh, 0, 0)),
            pl.BlockSpec(f1.shape, lambda b, l: (0, 0), pipeline_mode=pl.Buffered(1)),
            pl.BlockSpec(h2.shape, lambda b, l: (0, 0, 0), pipeline_mode=pl.Buffered(1)),
        ],
        out_specs=pl.BlockSpec((None, n2 * po, cb * groups), lambda b, l: (b, 0, l)),
        scratch_shapes=[pltpu.VMEM((groups, 2, cb // LANES, n1 * pa, LANES), F32)]
        + [pltpu.VMEM((groups, 2, cb // LANES, seq, LANES), F32)] * split_mid,
        compiler_params=_params("parallel", "arbitrary"),
        name="fourier",
    )(h.reshape(batch, seq, d), fc, f1, h2)
    return out.reshape(batch * n2 * po, d)


def _proj_body(n1, po, x_ref, m_ref, w_ref, o_ref):
    if po == n1:
        m = m_ref[...]
    else:
        m = jnp.concatenate([m_ref[q * po:q * po + n1, :]
                             for q in range(m_ref.shape[0] // po)], axis=0)
    o_ref[...] = x_ref[...] + jnp.dot(m.astype(BF16), w_ref[...],
                                      preferred_element_type=F32)


def _proj_residual(x, mixed, cfg, w, layer, *, tm=512):
    t, d = x.shape
    mrows = tm // cfg.n1 * cfg.pitch_o
    return pl.pallas_call(
        functools.partial(_proj_body, cfg.n1, cfg.pitch_o),
        out_shape=jax.ShapeDtypeStruct((t, d), F32),
        grid=(t // tm,),
        in_specs=[pl.BlockSpec((tm, d), lambda i: (i, 0)),
                  pl.BlockSpec((mrows, d), lambda i: (i, 0)),
                  pl.BlockSpec((None, d, d), lambda i: (layer, 0, 0),
                               pipeline_mode=pl.Buffered(1))],
        out_specs=pl.BlockSpec((tm, d), lambda i: (i, 0)),
        compiler_params=_params("parallel"),
        name="proj_residual",
    )(x, mixed, w)


_FFT_CONFIG = {
    16384: FftCfg(seq=16384, n1=128, n2=128, kb=1, cb=128, pitch_a=136,
                  pitch_o=136, unroll=16, h_buffers=1, groups=1, split_mid=False),
    2048: FftCfg(seq=2048, n1=128, n2=16, kb=8, cb=256, pitch_a=24,
                 pitch_o=128, unroll=16, h_buffers=2, groups=2, split_mid=True),
}


def _trunk(x3, w):
    batch, seq, d = x3.shape
    x = x3.reshape(batch * seq, d)
    depth = w["norm_ffn"].shape[0]
    h = None
    for i in range(depth):
        j = i // 2
        if i % 2 == 0:
            x = _conv_mixer(x, seq, w["norm_mix"], w["a_w_in"], w["a_conv_w"],
                            w["a_w_out"], j)
        else:
            cfg = _FFT_CONFIG[seq]
            mixed = _fourier(h, batch, cfg)
            x = _proj_residual(x, mixed, cfg, w["f_w_out"], j)
        mlp = functools.partial(_mlp, x, w["norm_ffn"], w["w_up"], w["w_down"], i)
        if i == depth - 1:
            x = mlp("only", w["final_norm"], 0)
        elif (i + 1) % 2 == 1:
            x, h = mlp("also", w["norm_mix"], i + 1)
        else:
            x = mlp()
    return x.reshape(batch, seq, d)


def kernel(x_prompt, x_sample, norm_mix, a_w_in, a_conv_w, a_w_out, f_w_out,
           norm_ffn, w_up, w_down, final_norm):
    d = x_prompt.shape[-1]
    w = dict(
        norm_mix=norm_mix.reshape(-1, 1, d),
        norm_ffn=norm_ffn.reshape(-1, 1, d),
        final_norm=final_norm.reshape(1, 1, d),
        a_w_in=a_w_in.astype(BF16),
        a_conv_w=a_conv_w,
        a_w_out=a_w_out.astype(BF16),
        f_w_out=f_w_out.astype(BF16),
        w_up=w_up.astype(BF16),
        w_down=w_down.astype(BF16),
    )
    return _trunk(x_prompt, w), _trunk(x_sample, w)
```

```python
import functools
from typing import NamedTuple

import jax
import jax.numpy as jnp
import numpy as np
from jax import lax
from jax.experimental import pallas as pl
from jax.experimental.pallas import tpu as pltpu

F32 = jnp.float32
BF16 = jnp.bfloat16

RMS_EPS = 1e-6
N_FGROUPS = 8
LANES = 128
BF16_ROWS = 16
VMEM_BYTES_V7X = 64 * 1024 * 1024
VMEM_LIMIT_BYTES = VMEM_BYTES_V7X - 4 * 1024 * 1024

TOKEN_TILE = 512
MLP_FF_TILE = 2048
CONV_CH_TILE = 1024


def _params(*sem):
    return pltpu.CompilerParams(dimension_semantics=sem,
                                vmem_limit_bytes=VMEM_LIMIT_BYTES)


def _rms(x, g):
    ms = jnp.mean(x * x, axis=-1, keepdims=True)
    return (x * lax.rsqrt(ms + RMS_EPS)) * g


def _mlp_body(post, x_ref, g_ref, wu_ref, wd_ref, *rest):
    if post is None:
        acc_ref, h_ref = rest
    elif post == "also":
        g2_ref, acc_ref, n_ref, h_ref = rest
    else:
        g2_ref, n_ref, h_ref, acc_ref = rest
    j = pl.program_id(1)
    last_j = pl.num_programs(1) - 1

    def step(first, last):
        if first:
            x = x_ref[...]
            h = _rms(x, g_ref[...]).astype(BF16)
            h_ref[...] = h
        else:
            h = h_ref[...]
        u = jnp.dot(h, wu_ref[...], preferred_element_type=F32)
        u = jnp.maximum(u, 0.0)
        u = (u * u).astype(BF16)
        y = (x_ref[...] if first else acc_ref[...]) + jnp.dot(
            u, wd_ref[...], preferred_element_type=F32)
        if not (last and post == "only"):
            acc_ref[...] = y
        if last and post is not None:
            n_ref[...] = _rms(y, g2_ref[...]).astype(n_ref.dtype)

    pl.when(j == 0)(functools.partial(step, True, False))
    if post is None:
        pl.when(j > 0)(functools.partial(step, False, False))
    else:
        pl.when((j > 0) & (j < last_j))(functools.partial(step, False, False))
        pl.when(j == last_j)(functools.partial(step, False, True))


def _mlp(x, g, w_up, w_down, layer, post=None, g2=None, layer2=0, *, tm=TOKEN_TILE,
         tf=MLP_FF_TILE):
    t, d = x.shape
    ff = w_up.shape[-1]
    assert ff // tf >= 2
    tile = pl.BlockSpec((tm, d), lambda i, j: (i, 0))
    in_specs = [
        tile,
        pl.BlockSpec((None, 1, d), lambda i, j: (layer, 0, 0)),
        pl.BlockSpec((None, d, tf), lambda i, j: (layer, 0, j)),
        pl.BlockSpec((None, tf, d), lambda i, j: (layer, j, 0)),
    ]
    args = [x, g, w_up, w_down]
    scratch = [pltpu.VMEM((tm, d), BF16)]
    x_new = jax.ShapeDtypeStruct((t, d), F32)
    if post is None:
        out_shape, out_specs = x_new, tile
    else:
        in_specs.append(pl.BlockSpec((None, 1, d), lambda i, j: (layer2, 0, 0)))
        args.append(g2)
        if post == "also":
            out_shape = (x_new, jax.ShapeDtypeStruct((t, d), BF16))
            out_specs = (tile, tile)
        else:
            out_shape, out_specs = x_new, tile
            scratch.append(pltpu.VMEM((tm, d), F32))
    return pl.pallas_call(
        functools.partial(_mlp_body, post),
        out_shape=out_shape,
        grid=(t // tm, ff // tf),
        in_specs=in_specs,
        out_specs=out_specs,
        scratch_shapes=scratch,
        compiler_params=_params("parallel", "arbitrary"),
        name="mlp",
    )(*args)


HALO = BF16_ROWS


def _conv_body(seq, tm, x_ref, xp_ref, xn_ref, g_ref, wb_ref, wc_ref, wv_ref,
               cw_ref, wo_ref, o_ref, h_ref):
    i = pl.program_id(0)
    j = pl.program_id(1)
    rows = tm + 2 * HALO

    def step(first):
        if first:
            g = g_ref[...]
            keep_prev = ((i * tm) % seq != 0).astype(F32)
            keep_next = (((i + 1) * tm) % seq != 0).astype(F32)
            h = jnp.concatenate(
                [(_rms(xp_ref[...], g) * keep_prev).astype(BF16),
                 _rms(x_ref[...], g).astype(BF16),
                 (_rms(xn_ref[...], g) * keep_next).astype(BF16)], axis=0)
            h_ref[...] = h
        else:
            h = h_ref[...]
        c = jnp.dot(h, wc_ref[...], preferred_element_type=F32)
        v = jnp.dot(h, wv_ref[...], preferred_element_type=F32)
        b = jnp.dot(h[HALO:HALO + tm], wb_ref[...], preferred_element_type=F32)
        u = c * v
        cw = cw_ref[...]
        u_prev = pltpu.roll(u, 1, axis=0)[HALO:HALO + tm]
        u_next = pltpu.roll(u, rows - 1, axis=0)[HALO:HALO + tm]
        conv = cw[0:1] * u_prev + cw[1:2] * u[HALO:HALO + tm] + cw[2:3] * u_next
        gated = (b * conv).astype(BF16)
        o_ref[...] = (x_ref[...] if first else o_ref[...]) + jnp.dot(
            gated, wo_ref[...], preferred_element_type=F32)

    pl.when(j == 0)(functools.partial(step, True))
    pl.when(j > 0)(functools.partial(step, False))


def _conv_mixer(x, seq, g, w_in, conv_w, w_out, layer, *, tm=TOKEN_TILE,
                tc=CONV_CH_TILE):
    t, d = x.shape
    cdim = conv_w.shape[-1]
    nc = cdim // tc
    hb = tm // HALO
    last = t // HALO - 1
    return pl.pallas_call(
        functools.partial(_conv_body, seq, tm),
        out_shape=jax.ShapeDtypeStruct((t, d), F32),
        grid=(t // tm, nc),
        in_specs=[
            pl.BlockSpec((tm, d), lambda i, j: (i, 0)),
            pl.BlockSpec((HALO, d), lambda i, j: (jnp.maximum(i * hb - 1, 0), 0)),
            pl.BlockSpec((HALO, d), lambda i, j: (jnp.minimum((i + 1) * hb, last), 0)),
            pl.BlockSpec((None, 1, d), lambda i, j: (2 * layer, 0, 0)),
            pl.BlockSpec((None, d, tc), lambda i, j: (layer, 0, j)),
            pl.BlockSpec((None, d, tc), lambda i, j: (layer, 0, nc + j)),
            pl.BlockSpec((None, d, tc), lambda i, j: (layer, 0, 2 * nc + j)),
            pl.BlockSpec((None, 3, tc), lambda i, j: (layer, 0, j)),
            pl.BlockSpec((None, tc, d), lambda i, j: (layer, j, 0)),
        ],
        out_specs=pl.BlockSpec((tm, d), lambda i, j: (i, 0)),
        scratch_shapes=[pltpu.VMEM((tm + 2 * HALO, d), BF16)],
        compiler_params=_params("parallel", "arbitrary"),
        name="conv_mixer",
    )(x, x, x, g, w_in, w_in, w_in, conv_w, w_out)


def _dft_tables(seq, n1, n2, kb, cb, gdim, split_mid):
    c = np.arange(gdim)
    ang = 2.0 * np.pi * ((c[:, None] * c[None, :]) % gdim) / gdim
    fre, fim = np.cos(ang) / np.sqrt(gdim), -np.sin(ang) / np.sqrt(gdim)
    nh = gdim // cb
    fc = np.stack([np.concatenate([fre[:, q * cb:(q + 1) * cb],
                                   fim[:, q * cb:(q + 1) * cb]], axis=1)
                   for q in range(nh)])

    k = np.arange(n1)
    ang = 2.0 * np.pi * ((k[:, None] * k[None, :]) % n1) / n1
    gr, gi = np.cos(ang) / np.sqrt(n1), -np.sin(ang) / np.sqrt(n1)
    f1 = np.block([[gr, -gi], [gi, gr]])

    na = n1 // kb
    m = kb * n2
    h2 = np.zeros((na, m, 2 * m))
    nn = np.arange(n2)
    for a in range(na):
        for i in range(kb):
            k1 = a * kb + i
            kk = k1 + n1 * np.arange(n2)
            ang = 2.0 * np.pi * ((kk[:, None] * nn[None, :]) % seq) / seq
            wr, wi = np.cos(ang) / np.sqrt(n2), -np.sin(ang) / np.sqrt(n2)
            rows = np.arange(n2) * kb + i
            cols = nn * kb + i if split_mid else i * n2 + nn
            h2[a, rows[:, None], cols[None, :]] = wr
            h2[a, rows[:, None], m + cols[None, :]] = -wi
    return (np.asarray(fc, np.float32), np.asarray(f1, np.float32),
            np.asarray(h2, np.float32))


class FftCfg(NamedTuple):
    seq: int
    n1: int
    n2: int
    kb: int
    cb: int
    pitch_a: int
    pitch_o: int
    unroll: int
    h_buffers: int
    groups: int
    split_mid: bool


def _loop(n, body, unroll):
    if unroll >= n:
        for i in range(n):
            body(i)
    else:
        lax.fori_loop(0, n, lambda i, c: (body(i), c)[1], 0, unroll=unroll)


def _aligned(x, m):
    return x if isinstance(x, int) else pl.multiple_of(x, m)


def _fft_body(cfg, h_ref, fc_ref, f1_ref, h2_ref, o_ref, a_ref, *b_ref):
    seq, n1, n2, kb, cb, pa, po, unroll, _, groups, split_mid = cfg
    mid_ref = b_ref[0] if split_mid else a_ref
    nl = cb // LANES
    gdim = h_ref.shape[1] // groups
    rc = max(n2, min(seq, 512))
    per = rc // n2

    def chan(r):
        for g in range(groups):
            w = jnp.dot(h_ref[pl.ds(_aligned(r * rc, rc), rc), g * gdim:(g + 1) * gdim],
                        fc_ref[...], preferred_element_type=F32)
            for q in range(per if pa != n2 else 1):
                size = n2 if pa != n2 else rc
                dst = pl.ds(_aligned((r * per + q) * pa, 8), size)
                for p in range(2):
                    for s in range(nl):
                        lo = p * cb + s * LANES
                        a_ref[g, p, s, dst, :] = w[q * n2:q * n2 + size, lo:lo + LANES]
    _loop(seq // rc, chan, unroll)

    def load(ref, g, rows):
        return jnp.concatenate(
            [jnp.concatenate(
                [jnp.concatenate([ref[g, p, s, r, :] for r in rows], axis=0)
                 for s in range(nl)], axis=1)
             for p in range(2)], axis=0).astype(BF16)

    nb = max(1, 2 * LANES // cb)
    def stage1(t):
        rows = [pl.ds(t * nb + q, n1, stride=pa) for q in range(nb)]
        for g in range(groups):
            rhs = jnp.concatenate([load(a_ref, g, [r]) for r in rows], axis=1)
            y = jnp.dot(f1_ref[...], rhs, preferred_element_type=F32)
            for q in range(nb):
                dst = (pl.ds(_aligned((t * nb + q) * n1, n1), n1) if split_mid
                       else rows[q])
                for p in range(2):
                    for s in range(nl):
                        lo = q * cb + s * LANES
                        mid_ref[g, p, s, dst, :] = y[p * n1:(p + 1) * n1, lo:lo + LANES]
    _loop(n2 // nb, stage1, unroll)

    m = kb * n2
    def stage2(a):
        for g in range(groups):
            if split_mid:
                src = [pl.ds(i2 * n1 + _aligned(a * kb, kb), kb) for i2 in range(n2)]
            else:
                src = [pl.ds(_aligned(a * kb * pa, 8), m)]
            y = jnp.dot(h2_ref[a], load(mid_ref, g, src),
                        preferred_element_type=F32)
            cols = slice(g * cb, (g + 1) * cb)
            if kb == 1:
                o_ref[pl.ds(a, n2, stride=po), cols] = y
            else:
                for k2 in range(n2):
                    o_ref[pl.ds(_aligned(a * kb, kb) + k2 * po, kb), cols] = (
                        y[k2 * kb:(k2 + 1) * kb])
    _loop(n1 // kb, stage2, unroll)

    if po != n1:
        for k2 in range(n2):
            o_ref[k2 * po + n1:(k2 + 1) * po, :] = jnp.zeros(
                (po - n1, groups * cb), F32)


def _fourier(h, batch, cfg):
    t, d = h.shape
    seq, n1, n2, kb, cb, pa, po, _, _, groups, split_mid = cfg
    gdim = d // N_FGROUPS
    nh = gdim // cb
    assert kb == 1 or pa == n2 or split_mid
    assert groups == 1 or nh == 1
    assert not split_mid or kb % 8 == 0
    fc, f1, h2 = (jnp.asarray(tab).astype(BF16)
                  for tab in _dft_tables(seq, n1, n2, kb, cb, gdim, split_mid))
    out = pl.pallas_call(
        functools.partial(_fft_body, cfg),
        out_shape=jax.ShapeDtypeStruct((batch, n2 * po, d), F32),
        grid=(batch, d // (cb * groups)),
        in_specs=[
            pl.BlockSpec((None, seq, gdim * groups), lambda b, l: (b, 0, l // nh),
                         pipeline_mode=pl.Buffered(cfg.h_buffers)),
            pl.BlockSpec((None, gdim, 2 * cb), lambda b, l: (l % nh, 0, 0)),
            pl.BlockSpec(f1.shape, lambda b, l: (0, 0), pipeline_mode=pl.Buffered(1)),
            pl.BlockSpec(h2.shape, lambda b, l: (0, 0, 0), pipeline_mode=pl.Buffered(1)),
        ],
        out_specs=pl.BlockSpec((None, n2 * po, cb * groups), lambda b, l: (b, 0, l)),
        scratch_shapes=[pltpu.VMEM((groups, 2, cb // LANES, n1 * pa, LANES), F32)]
        + [pltpu.VMEM((groups, 2, cb // LANES, seq, LANES), F32)] * split_mid,
        compiler_params=_params("parallel", "arbitrary"),
        name="fourier",
    )(h.reshape(batch, seq, d), fc, f1, h2)
    return out.reshape(batch * n2 * po, d)


def _proj_body(n1, po, x_ref, m_ref, w_ref, o_ref):
    if po == n1:
        m = m_ref[...]
    else:
        m = jnp.concatenate([m_ref[q * po:q * po + n1, :]
                             for q in range(m_ref.shape[0] // po)], axis=0)
    o_ref[...] = x_ref[...] + jnp.dot(m.astype(BF16), w_ref[...],
                                      preferred_element_type=F32)


def _proj_residual(x, mixed, cfg, w, layer, *, tm=TOKEN_TILE):
    t, d = x.shape
    mrows = tm // cfg.n1 * cfg.pitch_o
    return pl.pallas_call(
        functools.partial(_proj_body, cfg.n1, cfg.pitch_o),
        out_shape=jax.ShapeDtypeStruct((t, d), F32),
        grid=(t // tm,),
        in_specs=[pl.BlockSpec((tm, d), lambda i: (i, 0)),
                  pl.BlockSpec((mrows, d), lambda i: (i, 0)),
                  pl.BlockSpec((None, d, d), lambda i: (layer, 0, 0),
                               pipeline_mode=pl.Buffered(1))],
        out_specs=pl.BlockSpec((tm, d), lambda i: (i, 0)),
        compiler_params=_params("parallel"),
        name="proj_residual",
    )(x, mixed, w)


_FFT_CONFIG = {
    16384: FftCfg(seq=16384, n1=128, n2=128, kb=1, cb=128, pitch_a=136,
                  pitch_o=136, unroll=16, h_buffers=1, groups=1, split_mid=False),
    2048: FftCfg(seq=2048, n1=128, n2=16, kb=8, cb=256, pitch_a=24,
                 pitch_o=128, unroll=16, h_buffers=2, groups=2, split_mid=True),
}


def _trunk(x3, w):
    batch, seq, d = x3.shape
    x = x3.reshape(batch * seq, d)
    depth = w["norm_ffn"].shape[0]
    h = None
    for i in range(depth):
        j = i // 2
        if i % 2 == 0:
            x = _conv_mixer(x, seq, w["norm_mix"], w["a_w_in"], w["a_conv_w"],
                            w["a_w_out"], j)
        else:
            cfg = _FFT_CONFIG[seq]
            mixed = _fourier(h, batch, cfg)
            x = _proj_residual(x, mixed, cfg, w["f_w_out"], j)
        mlp = functools.partial(_mlp, x, w["norm_ffn"], w["w_up"], w["w_down"], i)
        if i == depth - 1:
            x = mlp("only", w["final_norm"], 0)
        elif (i + 1) % 2 == 1:
            x, h = mlp("also", w["norm_mix"], i + 1)
        else:
            x = mlp()
    return x.reshape(batch, seq, d)


def kernel(x_prompt, x_sample, norm_mix, a_w_in, a_conv_w, a_w_out, f_w_out,
           norm_ffn, w_up, w_down, final_norm):
    d = x_prompt.shape[-1]
    w = dict(
        norm_mix=norm_mix.reshape(-1, 1, d),
        norm_ffn=norm_ffn.reshape(-1, 1, d),
        final_norm=final_norm.reshape(1, 1, d),
        a_w_in=a_w_in.astype(BF16),
        a_conv_w=a_conv_w,
        a_w_out=a_w_out.astype(BF16),
        f_w_out=f_w_out.astype(BF16),
        w_up=w_up.astype(BF16),
        w_down=w_down.astype(BF16),
    )
    return _trunk(x_prompt, w), _trunk(x_sample, w)
```

```python
import functools
from typing import NamedTuple

import jax
import jax.numpy as jnp
import numpy as np
from jax import lax
from jax.experimental import pallas as pl
from jax.experimental.pallas import tpu as pltpu

F32 = jnp.float32
BF16 = jnp.bfloat16

RMS_EPS = 1e-6
N_FGROUPS = 8
LANES = 128
BF16_ROWS = 16
VMEM_BYTES_V7X = 64 * 1024 * 1024
VMEM_LIMIT_BYTES = VMEM_BYTES_V7X - 4 * 1024 * 1024

TOKEN_TILE = 512
MLP_FF_TILE = 2048
CONV_CH_TILE = 1024


def _params(*sem):
    return pltpu.CompilerParams(dimension_semantics=sem,
                                vmem_limit_bytes=VMEM_LIMIT_BYTES)


def _rms(x, g):
    ms = jnp.mean(x * x, axis=-1, keepdims=True)
    return (x * lax.rsqrt(ms + RMS_EPS)) * g


def _mlp_body(post, x_ref, g_ref, wu_ref, wd_ref, *rest):
    if post is None:
        acc_ref, h_ref = rest
    elif post == "also":
        g2_ref, acc_ref, n_ref, h_ref = rest
    else:
        g2_ref, n_ref, h_ref, acc_ref = rest
    j = pl.program_id(1)
    last_j = pl.num_programs(1) - 1

    def step(first, last):
        if first:
            x = x_ref[...]
            h = _rms(x, g_ref[...]).astype(BF16)
            h_ref[...] = h
        else:
            h = h_ref[...]
        u = jnp.dot(h, wu_ref[...], preferred_element_type=F32)
        u = jnp.maximum(u, 0.0)
        u = (u * u).astype(BF16)
        y = (x_ref[...] if first else acc_ref[...]) + jnp.dot(
            u, wd_ref[...], preferred_element_type=F32)
        if not (last and post == "only"):
            acc_ref[...] = y
        if last and post is not None:
            n_ref[...] = _rms(y, g2_ref[...]).astype(n_ref.dtype)

    pl.when(j == 0)(functools.partial(step, True, False))
    if post is None:
        pl.when(j > 0)(functools.partial(step, False, False))
    else:
        pl.when((j > 0) & (j < last_j))(functools.partial(step, False, False))
        pl.when(j == last_j)(functools.partial(step, False, True))


def _mlp(x, g, w_up, w_down, layer, post=None, g2=None, layer2=0, *, tm=TOKEN_TILE,
         tf=MLP_FF_TILE):
    t, d = x.shape
    ff = w_up.shape[-1]
    assert ff // tf >= 2
    tile = pl.BlockSpec((tm, d), lambda i, j: (i, 0))
    in_specs = [
        tile,
        pl.BlockSpec((None, 1, d), lambda i, j: (layer, 0, 0)),
        pl.BlockSpec((None, d, tf), lambda i, j: (layer, 0, j)),
        pl.BlockSpec((None, tf, d), lambda i, j: (layer, j, 0)),
    ]
    args = [x, g, w_up, w_down]
    scratch = [pltpu.VMEM((tm, d), BF16)]
    x_new = jax.ShapeDtypeStruct((t, d), F32)
    if post is None:
        out_shape, out_specs = x_new, tile
    else:
        in_specs.append(pl.BlockSpec((None, 1, d), lambda i, j: (layer2, 0, 0)))
        args.append(g2)
        if post == "also":
            out_shape = (x_new, jax.ShapeDtypeStruct((t, d), BF16))
            out_specs = (tile, tile)
        else:
            out_shape, out_specs = x_new, tile
            scratch.append(pltpu.VMEM((tm, d), F32))
    return pl.pallas_call(
        functools.partial(_mlp_body, post),
        out_shape=out_shape,
        grid=(t // tm, ff // tf),
        in_specs=in_specs,
        out_specs=out_specs,
        scratch_shapes=scratch,
        compiler_params=_params("parallel", "arbitrary"),
        name="mlp",
    )(*args)


HALO = BF16_ROWS


def _conv_body(seq, tm, x_ref, xp_ref, xn_ref, g_ref, wb_ref, wc_ref, wv_ref,
               cw_ref, wo_ref, o_ref, h_ref):
    i = pl.program_id(0)
    j = pl.program_id(1)
    rows = tm + 2 * HALO

    def step(first):
        if first:
            g = g_ref[...]
            keep_prev = ((i * tm) % seq != 0).astype(F32)
            keep_next = (((i + 1) * tm) % seq != 0).astype(F32)
            h = jnp.concatenate(
                [(_rms(xp_ref[...], g) * keep_prev).astype(BF16),
                 _rms(x_ref[...], g).astype(BF16),
                 (_rms(xn_ref[...], g) * keep_next).astype(BF16)], axis=0)
            h_ref[...] = h
        else:
            h = h_ref[...]
        c = jnp.dot(h, wc_ref[...], preferred_element_type=F32)
        v = jnp.dot(h, wv_ref[...], preferred_element_type=F32)
        b = jnp.dot(h[HALO:HALO + tm], wb_ref[...], preferred_element_type=F32)
        u = c * v
        cw = cw_ref[...]
        u_prev = pltpu.roll(u, 1, axis=0)[HALO:HALO + tm]
        u_next = pltpu.roll(u, rows - 1, axis=0)[HALO:HALO + tm]
        conv = cw[0:1] * u_prev + cw[1:2] * u[HALO:HALO + tm] + cw[2:3] * u_next
        gated = (b * conv).astype(BF16)
        o_ref[...] = (x_ref[...] if first else o_ref[...]) + jnp.dot(
            gated, wo_ref[...], preferred_element_type=F32)

    pl.when(j == 0)(functools.partial(step, True))
    pl.when(j > 0)(functools.partial(step, False))


def _conv_mixer(x, seq, g, w_in, conv_w, w_out, layer, *, tm=TOKEN_TILE,
                tc=CONV_CH_TILE):
    t, d = x.shape
    cdim = conv_w.shape[-1]
    nc = cdim // tc
    hb = tm // HALO
    last = t // HALO - 1
    return pl.pallas_call(
        functools.partial(_conv_body, seq, tm),
        out_shape=jax.ShapeDtypeStruct((t, d), F32),
        grid=(t // tm, nc),
        in_specs=[
            pl.BlockSpec((tm, d), lambda i, j: (i, 0)),
            pl.BlockSpec((HALO, d), lambda i, j: (jnp.maximum(i * hb - 1, 0), 0)),
            pl.BlockSpec((HALO, d), lambda i, j: (jnp.minimum((i + 1) * hb, last), 0)),
            pl.BlockSpec((None, 1, d), lambda i, j: (2 * layer, 0, 0)),
            pl.BlockSpec((None, d, tc), lambda i, j: (layer, 0, j)),
            pl.BlockSpec((None, d, tc), lambda i, j: (layer, 0, nc + j)),
            pl.BlockSpec((None, d, tc), lambda i, j: (layer, 0, 2 * nc + j)),
            pl.BlockSpec((None, 3, tc), lambda i, j: (layer, 0, j)),
            pl.BlockSpec((None, tc, d), lambda i, j: (layer, j, 0)),
        ],
        out_specs=pl.BlockSpec((tm, d), lambda i, j: (i, 0)),
        scratch_shapes=[pltpu.VMEM((tm + 2 * HALO, d), BF16)],
        compiler_params=_params("parallel", "arbitrary"),
        name="conv_mixer",
    )(x, x, x, g, w_in, w_in, w_in, conv_w, w_out)


def _dft_tables(seq, n1, n2, kb, cb, gdim, split_mid):
    c = np.arange(gdim)
    ang = 2.0 * np.pi * ((c[:, None] * c[None, :]) % gdim) / gdim
    fre, fim = np.cos(ang) / np.sqrt(gdim), -np.sin(ang) / np.sqrt(gdim)
    nh = gdim // cb
    fc = np.stack([np.concatenate([fre[:, q * cb:(q + 1) * cb],
                                   fim[:, q * cb:(q + 1) * cb]], axis=1)
                   for q in range(nh)])

    k = np.arange(n1)
    ang = 2.0 * np.pi * ((k[:, None] * k[None, :]) % n1) / n1
    gr, gi = np.cos(ang) / np.sqrt(n1), -np.sin(ang) / np.sqrt(n1)
    f1 = np.block([[gr, -gi], [gi, gr]])

    na = n1 // kb
    m = kb * n2
    h2 = np.zeros((na, m, 2 * m))
    nn = np.arange(n2)
    for a in range(na):
        for i in range(kb):
            k1 = a * kb + i
            kk = k1 + n1 * np.arange(n2)
            ang = 2.0 * np.pi * ((kk[:, None] * nn[None, :]) % seq) / seq
            wr, wi = np.cos(ang) / np.sqrt(n2), -np.sin(ang) / np.sqrt(n2)
            rows = np.arange(n2) * kb + i
            cols = nn * kb + i if split_mid else i * n2 + nn
            h2[a, rows[:, None], cols[None, :]] = wr
            h2[a, rows[:, None], m + cols[None, :]] = -wi
    return (np.asarray(fc, np.float32), np.asarray(f1, np.float32),
            np.asarray(h2, np.float32))


class FftCfg(NamedTuple):
    seq: int
    n1: int
    n2: int
    kb: int
    cb: int
    pitch_a: int
    pitch_o: int
    unroll: int
    h_buffers: int
    groups: int
    split_mid: bool


class _Idx(NamedTuple):
    blk: object
    span: int
    u: int

    def value(self):
        return self.blk * self.span + self.u

    def times(self, m):
        step = self.span * m
        base = self.blk * step
        if not isinstance(base, int):
            base = pl.multiple_of(base, min(step & -step, 1024))
        return base + self.u * m


def _loop(n, body, unroll):
    if unroll >= n:
        for u in range(n):
            body(_Idx(0, n, u))
    else:
        def trip(blk, carry):
            for u in range(unroll):
                body(_Idx(blk, unroll, u))
            return carry
        lax.fori_loop(0, n // unroll, trip, 0)


def _fft_body(cfg, h_ref, fc_ref, f1_ref, h2_ref, o_ref, a_ref, *b_ref):
    seq, n1, n2, kb, cb, pa, po, unroll, _, groups, split_mid = cfg
    mid_ref = b_ref[0] if split_mid else a_ref
    nl = cb // LANES
    gdim = h_ref.shape[1] // groups
    rc = max(n2, min(seq, 512))
    per = rc // n2

    def chan(r):
        for g in range(groups):
            w = jnp.dot(h_ref[pl.ds(r.times(rc), rc), g * gdim:(g + 1) * gdim],
                        fc_ref[...], preferred_element_type=F32)
            for q in range(per if pa != n2 else 1):
                size = n2 if pa != n2 else rc
                dst = pl.ds(r.times(per * pa) + q * pa, size)
                for p in range(2):
                    for s in range(nl):
                        lo = p * cb + s * LANES
                        a_ref[g, p, s, dst, :] = w[q * n2:q * n2 + size, lo:lo + LANES]
    _loop(seq // rc, chan, unroll)

    def load(ref, g, rows):
        return jnp.concatenate(
            [jnp.concatenate(
                [jnp.concatenate([ref[g, p, s, r, :] for r in rows], axis=0)
                 for s in range(nl)], axis=1)
             for p in range(2)], axis=0).astype(BF16)

    nb = max(1, 2 * LANES // cb)
    def stage1(t):
        rows = [pl.ds(t.value() * nb + q, n1, stride=pa) for q in range(nb)]
        for g in range(groups):
            rhs = jnp.concatenate([load(a_ref, g, [r]) for r in rows], axis=1)
            y = jnp.dot(f1_ref[...], rhs, preferred_element_type=F32)
            for q in range(nb):
                dst = (pl.ds(t.times(nb * n1) + q * n1, n1) if split_mid
                       else rows[q])
                for p in range(2):
                    for s in range(nl):
                        lo = q * cb + s * LANES
                        mid_ref[g, p, s, dst, :] = y[p * n1:(p + 1) * n1, lo:lo + LANES]
    _loop(n2 // nb, stage1, unroll)

    m = kb * n2
    def stage2(a):
        for g in range(groups):
            if split_mid:
                src = [pl.ds(i2 * n1 + a.times(kb), kb) for i2 in range(n2)]
            else:
                src = [pl.ds(a.times(kb * pa), m)]
            y = jnp.dot(h2_ref[a.value()], load(mid_ref, g, src),
                        preferred_element_type=F32)
            cols = slice(g * cb, (g + 1) * cb)
            if kb == 1:
                o_ref[pl.ds(a.value(), n2, stride=po), cols] = y
            else:
                for k2 in range(n2):
                    o_ref[pl.ds(a.times(kb) + k2 * po, kb), cols] = (
                        y[k2 * kb:(k2 + 1) * kb])
    _loop(n1 // kb, stage2, unroll)

    if po != n1:
        for k2 in range(n2):
            o_ref[k2 * po + n1:(k2 + 1) * po, :] = jnp.zeros(
                (po - n1, groups * cb), F32)


def _fourier(h, batch, cfg):
    t, d = h.shape
    seq, n1, n2, kb, cb, pa, po, _, _, groups, split_mid = cfg
    gdim = d // N_FGROUPS
    nh = gdim // cb
    assert kb == 1 or pa == n2 or split_mid
    assert groups == 1 or nh == 1
    assert not split_mid or kb % 8 == 0
    fc, f1, h2 = (jnp.asarray(tab).astype(BF16)
                  for tab in _dft_tables(seq, n1, n2, kb, cb, gdim, split_mid))
    out = pl.pallas_call(
        functools.partial(_fft_body, cfg),
        out_shape=jax.ShapeDtypeStruct((batch, n2 * po, d), F32),
        grid=(batch, d // (cb * groups)),
        in_specs=[
            pl.BlockSpec((None, seq, gdim * groups), lambda b, l: (b, 0, l // nh),
                         pipeline_mode=pl.Buffered(cfg.h_buffers)),
            pl.BlockSpec((None, gdim, 2 * cb), lambda b, l: (l % nh, 0, 0)),
            pl.BlockSpec(f1.shape, lambda b, l: (0, 0), pipeline_mode=pl.Buffered(1)),
            pl.BlockSpec(h2.shape, lambda b, l: (0, 0, 0), pipeline_mode=pl.Buffered(1)),
        ],
        out_specs=pl.BlockSpec((None, n2 * po, cb * groups), lambda b, l: (b, 0, l)),
        scratch_shapes=[pltpu.VMEM((groups, 2, cb // LANES, n1 * pa, LANES), F32)]
        + [pltpu.VMEM((groups, 2, cb // LANES, seq, LANES), F32)] * split_mid,
        compiler_params=_params("parallel", "arbitrary"),
        name="fourier",
    )(h.reshape(batch, seq, d), fc, f1, h2)
    return out.reshape(batch * n2 * po, d)


def _proj_body(n1, po, x_ref, m_ref, w_ref, o_ref):
    if po == n1:
        m = m_ref[...]
    else:
        m = jnp.concatenate([m_ref[q * po:q * po + n1, :]
                             for q in range(m_ref.shape[0] // po)], axis=0)
    o_ref[...] = x_ref[...] + jnp.dot(m.astype(BF16), w_ref[...],
                                      preferred_element_type=F32)


def _proj_residual(x, mixed, cfg, w, layer, *, tm=TOKEN_TILE):
    t, d = x.shape
    mrows = tm // cfg.n1 * cfg.pitch_o
    return pl.pallas_call(
        functools.partial(_proj_body, cfg.n1, cfg.pitch_o),
        out_shape=jax.ShapeDtypeStruct((t, d), F32),
        grid=(t // tm,),
        in_specs=[pl.BlockSpec((tm, d), lambda i: (i, 0)),
                  pl.BlockSpec((mrows, d), lambda i: (i, 0)),
                  pl.BlockSpec((None, d, d), lambda i: (layer, 0, 0),
                               pipeline_mode=pl.Buffered(1))],
        out_specs=pl.BlockSpec((tm, d), lambda i: (i, 0)),
        compiler_params=_params("parallel"),
        name="proj_residual",
    )(x, mixed, w)


_FFT_CONFIG = {
    16384: FftCfg(seq=16384, n1=128, n2=128, kb=1, cb=128, pitch_a=132,
                  pitch_o=132, unroll=16, h_buffers=1, groups=1, split_mid=False),
    2048: FftCfg(seq=2048, n1=128, n2=16, kb=8, cb=256, pitch_a=20,
                 pitch_o=128, unroll=16, h_buffers=2, groups=2, split_mid=True),
}


def _trunk(x3, w):
    batch, seq, d = x3.shape
    x = x3.reshape(batch * seq, d)
    depth = w["norm_ffn"].shape[0]
    h = None
    for i in range(depth):
        j = i // 2
        if i % 2 == 0:
            x = _conv_mixer(x, seq, w["norm_mix"], w["a_w_in"], w["a_conv_w"],
                            w["a_w_out"], j)
        else:
            cfg = _FFT_CONFIG[seq]
            mixed = _fourier(h, batch, cfg)
            x = _proj_residual(x, mixed, cfg, w["f_w_out"], j)
        mlp = functools.partial(_mlp, x, w["norm_ffn"], w["w_up"], w["w_down"], i)
        if i == depth - 1:
            x = mlp("only", w["final_norm"], 0)
        elif (i + 1) % 2 == 1:
            x, h = mlp("also", w["norm_mix"], i + 1)
        else:
            x = mlp()
    return x.reshape(batch, seq, d)


def kernel(x_prompt, x_sample, norm_mix, a_w_in, a_conv_w, a_w_out, f_w_out,
           norm_ffn, w_up, w_down, final_norm):
    d = x_prompt.shape[-1]
    w = dict(
        norm_mix=norm_mix.reshape(-1, 1, d),
        norm_ffn=norm_ffn.reshape(-1, 1, d),
        final_norm=final_norm.reshape(1, 1, d),
        a_w_in=a_w_in.astype(BF16),
        a_conv_w=a_conv_w,
        a_w_out=a_w_out.astype(BF16),
        f_w_out=f_w_out.astype(BF16),
        w_up=w_up.astype(BF16),
        w_down=w_down.astype(BF16),
    )
    return _trunk(x_prompt, w), _trunk(x_sample, w)
```

```python
import functools
from typing import NamedTuple

import jax
import jax.numpy as jnp
import numpy as np
from jax import lax
from jax.experimental import pallas as pl
from jax.experimental.pallas import tpu as pltpu

F32 = jnp.float32
BF16 = jnp.bfloat16

RMS_EPS = 1e-6
N_FGROUPS = 8
LANES = 128
BF16_ROWS = 16
VMEM_BYTES_V7X = 64 * 1024 * 1024
VMEM_LIMIT_BYTES = VMEM_BYTES_V7X - 4 * 1024 * 1024

TOKEN_TILE = 512
MLP_FF_TILE = 2048
CONV_CH_TILE = 1024


def _params(*sem):
    return pltpu.CompilerParams(dimension_semantics=sem,
                                vmem_limit_bytes=VMEM_LIMIT_BYTES)


def _rms(x, g):
    ms = jnp.mean(x * x, axis=-1, keepdims=True)
    return (x * lax.rsqrt(ms + RMS_EPS)) * g


def _mlp_body(post, x_ref, g_ref, wu_ref, wd_ref, *rest):
    if post is None:
        acc_ref, h_ref = rest
    elif post == "also":
        g2_ref, acc_ref, n_ref, h_ref = rest
    else:
        g2_ref, n_ref, h_ref, acc_ref = rest
    j = pl.program_id(1)
    last_j = pl.num_programs(1) - 1

    def step(first, last):
        if first:
            x = x_ref[...]
            h = _rms(x, g_ref[...]).astype(BF16)
            h_ref[...] = h
        else:
            h = h_ref[...]
        u = jnp.dot(h, wu_ref[...], preferred_element_type=F32)
        u = jnp.maximum(u, 0.0)
        u = (u * u).astype(BF16)
        y = (x_ref[...] if first else acc_ref[...]) + jnp.dot(
            u, wd_ref[...], preferred_element_type=F32)
        if not (last and post == "only"):
            acc_ref[...] = y
        if last and post is not None:
            n_ref[...] = _rms(y, g2_ref[...]).astype(n_ref.dtype)

    pl.when(j == 0)(functools.partial(step, True, False))
    if post is None:
        pl.when(j > 0)(functools.partial(step, False, False))
    else:
        pl.when((j > 0) & (j < last_j))(functools.partial(step, False, False))
        pl.when(j == last_j)(functools.partial(step, False, True))


def _mlp(x, g, w_up, w_down, layer, post=None, g2=None, layer2=0, *, tm=TOKEN_TILE,
         tf=MLP_FF_TILE):
    t, d = x.shape
    ff = w_up.shape[-1]
    assert ff // tf >= 2
    tile = pl.BlockSpec((tm, d), lambda i, j: (i, 0))
    in_specs = [
        tile,
        pl.BlockSpec((None, 1, d), lambda i, j: (layer, 0, 0)),
        pl.BlockSpec((None, d, tf), lambda i, j: (layer, 0, j)),
        pl.BlockSpec((None, tf, d), lambda i, j: (layer, j, 0)),
    ]
    args = [x, g, w_up, w_down]
    scratch = [pltpu.VMEM((tm, d), BF16)]
    x_new = jax.ShapeDtypeStruct((t, d), F32)
    if post is None:
        out_shape, out_specs = x_new, tile
    else:
        in_specs.append(pl.BlockSpec((None, 1, d), lambda i, j: (layer2, 0, 0)))
        args.append(g2)
        if post == "also":
            out_shape = (x_new, jax.ShapeDtypeStruct((t, d), BF16))
            out_specs = (tile, tile)
        else:
            out_shape, out_specs = x_new, tile
            scratch.append(pltpu.VMEM((tm, d), F32))
    return pl.pallas_call(
        functools.partial(_mlp_body, post),
        out_shape=out_shape,
        grid=(t // tm, ff // tf),
        in_specs=in_specs,
        out_specs=out_specs,
        scratch_shapes=scratch,
        compiler_params=_params("parallel", "arbitrary"),
        name="mlp",
    )(*args)


HALO = BF16_ROWS


def _conv_body(seq, tm, x_ref, xp_ref, xn_ref, g_ref, wb_ref, wc_ref, wv_ref,
               cw_ref, wo_ref, o_ref, h_ref):
    i = pl.program_id(0)
    j = pl.program_id(1)
    rows = tm + 2 * HALO

    def step(first):
        if first:
            g = g_ref[...]
            keep_prev = ((i * tm) % seq != 0).astype(F32)
            keep_next = (((i + 1) * tm) % seq != 0).astype(F32)
            h = jnp.concatenate(
                [(_rms(xp_ref[...], g) * keep_prev).astype(BF16),
                 _rms(x_ref[...], g).astype(BF16),
                 (_rms(xn_ref[...], g) * keep_next).astype(BF16)], axis=0)
            h_ref[...] = h
        else:
            h = h_ref[...]
        c = jnp.dot(h, wc_ref[...], preferred_element_type=F32)
        v = jnp.dot(h, wv_ref[...], preferred_element_type=F32)
        b = jnp.dot(h[HALO:HALO + tm], wb_ref[...], preferred_element_type=F32)
        u = c * v
        cw = cw_ref[...]
        u_prev = pltpu.roll(u, 1, axis=0)[HALO:HALO + tm]
        u_next = pltpu.roll(u, rows - 1, axis=0)[HALO:HALO + tm]
        conv = cw[0:1] * u_prev + cw[1:2] * u[HALO:HALO + tm] + cw[2:3] * u_next
        gated = (b * conv).astype(BF16)
        o_ref[...] = (x_ref[...] if first else o_ref[...]) + jnp.dot(
            gated, wo_ref[...], preferred_element_type=F32)

    pl.when(j == 0)(functools.partial(step, True))
    pl.when(j > 0)(functools.partial(step, False))


def _conv_mixer(x, seq, g, w_in, conv_w, w_out, layer, *, tm=TOKEN_TILE,
                tc=CONV_CH_TILE):
    t, d = x.shape
    cdim = conv_w.shape[-1]
    nc = cdim // tc
    hb = tm // HALO
    last = t // HALO - 1
    return pl.pallas_call(
        functools.partial(_conv_body, seq, tm),
        out_shape=jax.ShapeDtypeStruct((t, d), F32),
        grid=(t // tm, nc),
        in_specs=[
            pl.BlockSpec((tm, d), lambda i, j: (i, 0)),
            pl.BlockSpec((HALO, d), lambda i, j: (jnp.maximum(i * hb - 1, 0), 0)),
            pl.BlockSpec((HALO, d), lambda i, j: (jnp.minimum((i + 1) * hb, last), 0)),
            pl.BlockSpec((None, 1, d), lambda i, j: (2 * layer, 0, 0)),
            pl.BlockSpec((None, d, tc), lambda i, j: (layer, 0, j)),
            pl.BlockSpec((None, d, tc), lambda i, j: (layer, 0, nc + j)),
            pl.BlockSpec((None, d, tc), lambda i, j: (layer, 0, 2 * nc + j)),
            pl.BlockSpec((None, 3, tc), lambda i, j: (layer, 0, j)),
            pl.BlockSpec((None, tc, d), lambda i, j: (layer, j, 0)),
        ],
        out_specs=pl.BlockSpec((tm, d), lambda i, j: (i, 0)),
        scratch_shapes=[pltpu.VMEM((tm + 2 * HALO, d), BF16)],
        compiler_params=_params("parallel", "arbitrary"),
        name="conv_mixer",
    )(x, x, x, g, w_in, w_in, w_in, conv_w, w_out)


def _dft_tables(seq, n1, n2, kb, cb, gdim, split_mid):
    c = np.arange(gdim)
    ang = 2.0 * np.pi * ((c[:, None] * c[None, :]) % gdim) / gdim
    fre, fim = np.cos(ang) / np.sqrt(gdim), -np.sin(ang) / np.sqrt(gdim)
    nh = gdim // cb
    fc = np.stack([np.concatenate([fre[:, q * cb:(q + 1) * cb],
                                   fim[:, q * cb:(q + 1) * cb]], axis=1)
                   for q in range(nh)])

    k = np.arange(n1)
    ang = 2.0 * np.pi * ((k[:, None] * k[None, :]) % n1) / n1
    gr, gi = np.cos(ang) / np.sqrt(n1), -np.sin(ang) / np.sqrt(n1)
    f1 = np.block([[gr, -gi], [gi, gr]])

    na = n1 // kb
    m = kb * n2
    h2 = np.zeros((na, m, 2 * m))
    nn = np.arange(n2)
    for a in range(na):
        for i in range(kb):
            k1 = a * kb + i
            kk = k1 + n1 * np.arange(n2)
            ang = 2.0 * np.pi * ((kk[:, None] * nn[None, :]) % seq) / seq
            wr, wi = np.cos(ang) / np.sqrt(n2), -np.sin(ang) / np.sqrt(n2)
            rows = np.arange(n2) * kb + i
            cols = nn * kb + i if split_mid else i * n2 + nn
            h2[a, rows[:, None], cols[None, :]] = wr
            h2[a, rows[:, None], m + cols[None, :]] = -wi
    return (np.asarray(fc, np.float32), np.asarray(f1, np.float32),
            np.asarray(h2, np.float32))


class FftCfg(NamedTuple):
    seq: int
    n1: int
    n2: int
    kb: int
    cb: int
    pitch_a: int
    pitch_o: int
    unroll: int
    h_buffers: int
    groups: int
    split_mid: bool


class _Idx(NamedTuple):
    blk: object
    span: int
    u: int

    def value(self):
        return self.blk * self.span + self.u

    def times(self, m):
        step = self.span * m
        base = self.blk * step
        if not isinstance(base, int):
            base = pl.multiple_of(base, min(step & -step, 1024))
        return base + self.u * m


def _loop(n, body, unroll):
    if unroll >= n:
        for u in range(n):
            body(_Idx(0, n, u))
    else:
        def trip(blk, carry):
            for u in range(unroll):
                body(_Idx(blk, unroll, u))
            return carry
        lax.fori_loop(0, n // unroll, trip, 0)


def _fft_body(cfg, h_ref, fc_ref, f1_ref, h2_ref, o_ref, a_ref, *b_ref):
    seq, n1, n2, kb, cb, pa, po, unroll, _, groups, split_mid = cfg
    mid_ref = b_ref[0] if split_mid else a_ref
    nl = cb // LANES
    gdim = h_ref.shape[1] // groups
    rc = max(n2, min(seq, 512))
    per = rc // n2

    def chan(r):
        for g in range(groups):
            w = jnp.dot(h_ref[pl.ds(r.times(rc), rc), g * gdim:(g + 1) * gdim],
                        fc_ref[...], preferred_element_type=F32)
            for q in range(per if pa != n2 else 1):
                size = n2 if pa != n2 else rc
                dst = pl.ds(r.times(per * pa) + q * pa, size)
                for p in range(2):
                    for s in range(nl):
                        lo = p * cb + s * LANES
                        a_ref[g, p, s, dst, :] = w[q * n2:q * n2 + size, lo:lo + LANES]
    _loop(seq // rc, chan, unroll)

    def load(ref, g, rows):
        return jnp.concatenate(
            [jnp.concatenate(
                [jnp.concatenate([ref[g, p, s, r, :] for r in rows], axis=0)
                 for s in range(nl)], axis=1)
             for p in range(2)], axis=0).astype(BF16)

    nb = max(1, 2 * LANES // cb)
    def stage1(t):
        rows = [pl.ds(t.value() * nb + q, n1, stride=pa) for q in range(nb)]
        for g in range(groups):
            rhs = jnp.concatenate([load(a_ref, g, [r]) for r in rows], axis=1)
            y = jnp.dot(f1_ref[...], rhs, preferred_element_type=F32)
            for q in range(nb):
                dst = (pl.ds(t.times(nb * n1) + q * n1, n1) if split_mid
                       else rows[q])
                for p in range(2):
                    for s in range(nl):
                        lo = q * cb + s * LANES
                        mid_ref[g, p, s, dst, :] = y[p * n1:(p + 1) * n1, lo:lo + LANES]
    _loop(n2 // nb, stage1, unroll)

    m = kb * n2
    def stage2(a):
        for g in range(groups):
            if split_mid:
                src = [pl.ds(i2 * n1 + a.times(kb), kb) for i2 in range(n2)]
            else:
                src = [pl.ds(a.times(kb * pa), m)]
            y = jnp.dot(h2_ref[a.value()], load(mid_ref, g, src),
                        preferred_element_type=F32)
            cols = slice(g * cb, (g + 1) * cb)
            if kb == 1:
                o_ref[pl.ds(a.value(), n2, stride=po), cols] = y
            else:
                for k2 in range(n2):
                    o_ref[pl.ds(a.times(kb) + k2 * po, kb), cols] = (
                        y[k2 * kb:(k2 + 1) * kb])
    _loop(n1 // kb, stage2, unroll)

    if po != n1:
        for k2 in range(n2):
            o_ref[k2 * po + n1:(k2 + 1) * po, :] = jnp.zeros(
                (po - n1, groups * cb), F32)


def _fourier(h, batch, cfg):
    t, d = h.shape
    seq, n1, n2, kb, cb, pa, po, _, _, groups, split_mid = cfg
    gdim = d // N_FGROUPS
    nh = gdim // cb
    assert kb == 1 or pa == n2 or split_mid
    assert groups == 1 or nh == 1
    assert not split_mid or kb % 8 == 0
    fc, f1, h2 = (jnp.asarray(tab).astype(BF16)
                  for tab in _dft_tables(seq, n1, n2, kb, cb, gdim, split_mid))
    out = pl.pallas_call(
        functools.partial(_fft_body, cfg),
        out_shape=jax.ShapeDtypeStruct((batch, n2 * po, d), F32),
        grid=(batch, d // (cb * groups)),
        in_specs=[
            pl.BlockSpec((None, seq, gdim * groups), lambda b, l: (b, 0, l // nh),
                         pipeline_mode=pl.Buffered(cfg.h_buffers)),
            pl.BlockSpec((None, gdim, 2 * cb), lambda b, l: (l % nh, 0, 0)),
            pl.BlockSpec(f1.shape, lambda b, l: (0, 0), pipeline_mode=pl.Buffered(1)),
            pl.BlockSpec(h2.shape, lambda b, l: (0, 0, 0), pipeline_mode=pl.Buffered(1)),
        ],
        out_specs=pl.BlockSpec((None, n2 * po, cb * groups), lambda b, l: (b, 0, l)),
        scratch_shapes=[pltpu.VMEM((groups, 2, cb // LANES, n1 * pa, LANES), F32)]
        + [pltpu.VMEM((groups, 2, cb // LANES, seq, LANES), F32)] * split_mid,
        compiler_params=_params("parallel", "arbitrary"),
        name="fourier",
    )(h.reshape(batch, seq, d), fc, f1, h2)
    return out.reshape(batch * n2 * po, d)


def _proj_body(n1, po, x_ref, m_ref, w_ref, o_ref):
    if po == n1:
        m = m_ref[...]
    else:
        m = jnp.concatenate([m_ref[q * po:q * po + n1, :]
                             for q in range(m_ref.shape[0] // po)], axis=0)
    o_ref[...] = x_ref[...] + jnp.dot(m.astype(BF16), w_ref[...],
                                      preferred_element_type=F32)


def _proj_residual(x, mixed, cfg, w, layer, *, tm=TOKEN_TILE):
    t, d = x.shape
    mrows = tm // cfg.n1 * cfg.pitch_o
    return pl.pallas_call(
        functools.partial(_proj_body, cfg.n1, cfg.pitch_o),
        out_shape=jax.ShapeDtypeStruct((t, d), F32),
        grid=(t // tm,),
        in_specs=[pl.BlockSpec((tm, d), lambda i: (i, 0)),
                  pl.BlockSpec((mrows, d), lambda i: (i, 0)),
                  pl.BlockSpec((None, d, d), lambda i: (layer, 0, 0),
                               pipeline_mode=pl.Buffered(1))],
        out_specs=pl.BlockSpec((tm, d), lambda i: (i, 0)),
        compiler_params=_params("parallel"),
        name="proj_residual",
    )(x, mixed, w)


_FFT_CONFIG = {
    16384: FftCfg(seq=16384, n1=128, n2=128, kb=1, cb=128, pitch_a=132,
                  pitch_o=132, unroll=32, h_buffers=2, groups=1, split_mid=False),
    2048: FftCfg(seq=2048, n1=128, n2=16, kb=8, cb=256, pitch_a=20,
                 pitch_o=128, unroll=16, h_buffers=2, groups=2, split_mid=True),
}


def _trunk(x3, w):
    batch, seq, d = x3.shape
    x = x3.reshape(batch * seq, d)
    depth = w["norm_ffn"].shape[0]
    h = None
    for i in range(depth):
        j = i // 2
        if i % 2 == 0:
            x = _conv_mixer(x, seq, w["norm_mix"], w["a_w_in"], w["a_conv_w"],
                            w["a_w_out"], j)
        else:
            cfg = _FFT_CONFIG[seq]
            mixed = _fourier(h, batch, cfg)
            x = _proj_residual(x, mixed, cfg, w["f_w_out"], j)
        mlp = functools.partial(_mlp, x, w["norm_ffn"], w["w_up"], w["w_down"], i)
        if i == depth - 1:
            x = mlp("only", w["final_norm"], 0)
        elif (i + 1) % 2 == 1:
            x, h = mlp("also", w["norm_mix"], i + 1)
        else:
            x = mlp()
    return x.reshape(batch, seq, d)


def kernel(x_prompt, x_sample, norm_mix, a_w_in, a_conv_w, a_w_out, f_w_out,
           norm_ffn, w_up, w_down, final_norm):
    d = x_prompt.shape[-1]
    w = dict(
        norm_mix=norm_mix.reshape(-1, 1, d),
        norm_ffn=norm_ffn.reshape(-1, 1, d),
        final_norm=final_norm.reshape(1, 1, d),
        a_w_in=a_w_in.astype(BF16),
        a_conv_w=a_conv_w,
        a_w_out=a_w_out.astype(BF16),
        f_w_out=f_w_out.astype(BF16),
        w_up=w_up.astype(BF16),
        w_down=w_down.astype(BF16),
    )
    return _trunk(x_prompt, w), _trunk(x_sample, w)
```

```python
import functools
from typing import NamedTuple

import jax
import jax.numpy as jnp
import numpy as np
from jax import lax
from jax.experimental import pallas as pl
from jax.experimental.pallas import tpu as pltpu

F32 = jnp.float32
BF16 = jnp.bfloat16

RMS_EPS = 1e-6
N_FGROUPS = 8
LANES = 128
BF16_ROWS = 16
VMEM_BYTES_V7X = 64 * 1024 * 1024
VMEM_LIMIT_BYTES = VMEM_BYTES_V7X - 4 * 1024 * 1024

TOKEN_TILE = 512
MLP_FF_TILE = 2048
CONV_CH_TILE = 1024


def _params(*sem):
    return pltpu.CompilerParams(dimension_semantics=sem,
                                vmem_limit_bytes=VMEM_LIMIT_BYTES)


def _rms(x, g):
    ms = jnp.mean(x * x, axis=-1, keepdims=True)
    return (x * lax.rsqrt(ms + RMS_EPS)) * g


def _mlp_body(post, x_ref, g_ref, wu_ref, wd_ref, *rest):
    if post is None:
        acc_ref, h_ref = rest
    elif post == "also":
        g2_ref, acc_ref, n_ref, h_ref = rest
    else:
        g2_ref, n_ref, h_ref, acc_ref = rest
    j = pl.program_id(1)
    last_j = pl.num_programs(1) - 1

    def step(first, last):
        if first:
            x = x_ref[...]
            h = _rms(x, g_ref[...]).astype(BF16)
            h_ref[...] = h
        else:
            h = h_ref[...]
        u = jnp.dot(h, wu_ref[...], preferred_element_type=F32)
        u = jnp.maximum(u, 0.0)
        u = (u * u).astype(BF16)
        y = (x_ref[...] if first else acc_ref[...]) + jnp.dot(
            u, wd_ref[...], preferred_element_type=F32)
        if not (last and post == "only"):
            acc_ref[...] = y
        if last and post is not None:
            n_ref[...] = _rms(y, g2_ref[...]).astype(n_ref.dtype)

    pl.when(j == 0)(functools.partial(step, True, False))
    if post is None:
        pl.when(j > 0)(functools.partial(step, False, False))
    else:
        pl.when((j > 0) & (j < last_j))(functools.partial(step, False, False))
        pl.when(j == last_j)(functools.partial(step, False, True))


def _mlp(x, g, w_up, w_down, layer, post=None, g2=None, layer2=0, *, tm=TOKEN_TILE,
         tf=MLP_FF_TILE):
    t, d = x.shape
    ff = w_up.shape[-1]
    assert ff // tf >= 2
    tile = pl.BlockSpec((tm, d), lambda i, j: (i, 0))
    in_specs = [
        tile,
        pl.BlockSpec((None, 1, d), lambda i, j: (layer, 0, 0)),
        pl.BlockSpec((None, d, tf), lambda i, j: (layer, 0, j)),
        pl.BlockSpec((None, tf, d), lambda i, j: (layer, j, 0)),
    ]
    args = [x, g, w_up, w_down]
    scratch = [pltpu.VMEM((tm, d), BF16)]
    x_new = jax.ShapeDtypeStruct((t, d), F32)
    if post is None:
        out_shape, out_specs = x_new, tile
    else:
        in_specs.append(pl.BlockSpec((None, 1, d), lambda i, j: (layer2, 0, 0)))
        args.append(g2)
        if post == "also":
            out_shape = (x_new, jax.ShapeDtypeStruct((t, d), BF16))
            out_specs = (tile, tile)
        else:
            out_shape, out_specs = x_new, tile
            scratch.append(pltpu.VMEM((tm, d), F32))
    return pl.pallas_call(
        functools.partial(_mlp_body, post),
        out_shape=out_shape,
        grid=(t // tm, ff // tf),
        in_specs=in_specs,
        out_specs=out_specs,
        scratch_shapes=scratch,
        compiler_params=_params("parallel", "arbitrary"),
        name="mlp",
    )(*args)


HALO = 8


def _conv_body(seq, tm, x_ref, xp_ref, xn_ref, g_ref, wb_ref, wc_ref, wv_ref,
               cw_ref, wo_ref, o_ref, h_ref):
    i = pl.program_id(0)
    j = pl.program_id(1)
    rows = tm + 2 * HALO

    def step(first):
        if first:
            g = g_ref[...]
            keep_prev = ((i * tm) % seq != 0).astype(F32)
            keep_next = (((i + 1) * tm) % seq != 0).astype(F32)
            h = jnp.concatenate(
                [_rms(x_ref[...], g),
                 _rms(xn_ref[...], g) * keep_next,
                 _rms(xp_ref[...], g) * keep_prev], axis=0).astype(BF16)
            h_ref[...] = h
        else:
            h = h_ref[...]
        c = jnp.dot(h, wc_ref[...], preferred_element_type=F32)
        v = jnp.dot(h, wv_ref[...], preferred_element_type=F32)
        b = jnp.dot(h[:tm], wb_ref[...], preferred_element_type=F32)
        u = c * v
        cw = cw_ref[...]
        u_prev = pltpu.roll(u, 1, axis=0)[:tm]
        u_next = pltpu.roll(u, rows - 1, axis=0)[:tm]
        conv = cw[0:1] * u_prev + cw[1:2] * u[:tm] + cw[2:3] * u_next
        gated = (b * conv).astype(BF16)
        o_ref[...] = (x_ref[...] if first else o_ref[...]) + jnp.dot(
            gated, wo_ref[...], preferred_element_type=F32)

    pl.when(j == 0)(functools.partial(step, True))
    pl.when(j > 0)(functools.partial(step, False))


def _conv_mixer(x, seq, g, w_in, conv_w, w_out, layer, *, tm=TOKEN_TILE,
                tc=CONV_CH_TILE):
    t, d = x.shape
    cdim = conv_w.shape[-1]
    nc = cdim // tc
    hb = tm // HALO
    last = t // HALO - 1
    return pl.pallas_call(
        functools.partial(_conv_body, seq, tm),
        out_shape=jax.ShapeDtypeStruct((t, d), F32),
        grid=(t // tm, nc),
        in_specs=[
            pl.BlockSpec((tm, d), lambda i, j: (i, 0)),
            pl.BlockSpec((HALO, d), lambda i, j: (jnp.maximum(i * hb - 1, 0), 0)),
            pl.BlockSpec((HALO, d), lambda i, j: (jnp.minimum((i + 1) * hb, last), 0)),
            pl.BlockSpec((None, 1, d), lambda i, j: (2 * layer, 0, 0)),
            pl.BlockSpec((None, d, tc), lambda i, j: (layer, 0, j)),
            pl.BlockSpec((None, d, tc), lambda i, j: (layer, 0, nc + j)),
            pl.BlockSpec((None, d, tc), lambda i, j: (layer, 0, 2 * nc + j)),
            pl.BlockSpec((None, 3, tc), lambda i, j: (layer, 0, j)),
            pl.BlockSpec((None, tc, d), lambda i, j: (layer, j, 0)),
        ],
        out_specs=pl.BlockSpec((tm, d), lambda i, j: (i, 0)),
        scratch_shapes=[pltpu.VMEM((tm + 2 * HALO, d), BF16)],
        compiler_params=_params("parallel", "arbitrary"),
        name="conv_mixer",
    )(x, x, x, g, w_in, w_in, w_in, conv_w, w_out)


def _dft_tables(seq, n1, n2, kb, cb, gdim, split_mid):
    c = np.arange(gdim)
    ang = 2.0 * np.pi * ((c[:, None] * c[None, :]) % gdim) / gdim
    fre, fim = np.cos(ang) / np.sqrt(gdim), -np.sin(ang) / np.sqrt(gdim)
    nh = gdim // cb
    fc = np.stack([np.concatenate([fre[:, q * cb:(q + 1) * cb],
                                   fim[:, q * cb:(q + 1) * cb]], axis=1)
                   for q in range(nh)])

    k = np.arange(n1)
    ang = 2.0 * np.pi * ((k[:, None] * k[None, :]) % n1) / n1
    gr, gi = np.cos(ang) / np.sqrt(n1), -np.sin(ang) / np.sqrt(n1)
    f1 = np.block([[gr, -gi], [gi, gr]])

    na = n1 // kb
    m = kb * n2
    h2 = np.zeros((na, m, 2 * m))
    nn = np.arange(n2)
    for a in range(na):
        for i in range(kb):
            k1 = a * kb + i
            kk = k1 + n1 * np.arange(n2)
            ang = 2.0 * np.pi * ((kk[:, None] * nn[None, :]) % seq) / seq
            wr, wi = np.cos(ang) / np.sqrt(n2), -np.sin(ang) / np.sqrt(n2)
            rows = np.arange(n2) * kb + i
            cols = nn * kb + i if split_mid else i * n2 + nn
            h2[a, rows[:, None], cols[None, :]] = wr
            h2[a, rows[:, None], m + cols[None, :]] = -wi
    return (np.asarray(fc, np.float32), np.asarray(f1, np.float32),
            np.asarray(h2, np.float32))


class FftCfg(NamedTuple):
    seq: int
    n1: int
    n2: int
    kb: int
    cb: int
    pitch_a: int
    pitch_o: int
    unroll: int
    h_buffers: int
    groups: int
    split_mid: bool


class _Idx(NamedTuple):
    blk: object
    span: int
    u: int

    def value(self):
        return self.blk * self.span + self.u

    def times(self, m):
        step = self.span * m
        base = self.blk * step
        if not isinstance(base, int):
            base = pl.multiple_of(base, min(step & -step, 1024))
        return base + self.u * m


def _loop(n, body, unroll):
    if unroll >= n:
        for u in range(n):
            body(_Idx(0, n, u))
    else:
        def trip(blk, carry):
            for u in range(unroll):
                body(_Idx(blk, unroll, u))
            return carry
        lax.fori_loop(0, n // unroll, trip, 0)


def _fft_body(cfg, h_ref, fc_ref, f1_ref, h2_ref, o_ref, a_ref, *b_ref):
    seq, n1, n2, kb, cb, pa, po, unroll, _, groups, split_mid = cfg
    mid_ref = b_ref[0] if split_mid else a_ref
    nl = cb // LANES
    gdim = h_ref.shape[1] // groups
    rc = max(n2, min(seq, 512))
    per = rc // n2

    def chan(r):
        for g in range(groups):
            w = jnp.dot(h_ref[pl.ds(r.times(rc), rc), g * gdim:(g + 1) * gdim],
                        fc_ref[...], preferred_element_type=F32)
            for q in range(per if pa != n2 else 1):
                size = n2 if pa != n2 else rc
                dst = pl.ds(r.times(per * pa) + q * pa, size)
                for p in range(2):
                    for s in range(nl):
                        lo = p * cb + s * LANES
                        a_ref[g, p, s, dst, :] = w[q * n2:q * n2 + size, lo:lo + LANES]
    _loop(seq // rc, chan, unroll)

    def load(ref, g, rows):
        return jnp.concatenate(
            [jnp.concatenate(
                [jnp.concatenate([ref[g, p, s, r, :] for r in rows], axis=0)
                 for s in range(nl)], axis=1)
             for p in range(2)], axis=0).astype(BF16)

    nb = max(1, 2 * LANES // cb)
    def stage1(t):
        rows = [pl.ds(t.value() * nb + q, n1, stride=pa) for q in range(nb)]
        for g in range(groups):
            rhs = jnp.concatenate([load(a_ref, g, [r]) for r in rows], axis=1)
            y = jnp.dot(f1_ref[...], rhs, preferred_element_type=F32)
            for q in range(nb):
                dst = (pl.ds(t.times(nb * n1) + q * n1, n1) if split_mid
                       else rows[q])
                for p in range(2):
                    for s in range(nl):
                        lo = q * cb + s * LANES
                        mid_ref[g, p, s, dst, :] = y[p * n1:(p + 1) * n1, lo:lo + LANES]
    _loop(n2 // nb, stage1, unroll)

    m = kb * n2
    def stage2(a):
        for g in range(groups):
            if split_mid:
                src = [pl.ds(i2 * n1 + a.times(kb), kb) for i2 in range(n2)]
            else:
                src = [pl.ds(a.times(kb * pa), m)]
            y = jnp.dot(h2_ref[a.value()], load(mid_ref, g, src),
                        preferred_element_type=F32)
            cols = slice(g * cb, (g + 1) * cb)
            if kb == 1:
                o_ref[pl.ds(a.value(), n2, stride=po), cols] = y
            else:
                for k2 in range(n2):
                    o_ref[pl.ds(a.times(kb) + k2 * po, kb), cols] = (
                        y[k2 * kb:(k2 + 1) * kb])
    _loop(n1 // kb, stage2, unroll)

    if po != n1:
        for k2 in range(n2):
            o_ref[k2 * po + n1:(k2 + 1) * po, :] = jnp.zeros(
                (po - n1, groups * cb), F32)


def _fourier(h, batch, cfg):
    t, d = h.shape
    seq, n1, n2, kb, cb, pa, po, _, _, groups, split_mid = cfg
    gdim = d // N_FGROUPS
    nh = gdim // cb
    assert kb == 1 or pa == n2 or split_mid
    assert groups == 1 or nh == 1
    assert not split_mid or kb % 8 == 0
    fc, f1, h2 = (jnp.asarray(tab).astype(BF16)
                  for tab in _dft_tables(seq, n1, n2, kb, cb, gdim, split_mid))
    out = pl.pallas_call(
        functools.partial(_fft_body, cfg),
        out_shape=jax.ShapeDtypeStruct((batch, n2 * po, d), F32),
        grid=(batch, d // (cb * groups)),
        in_specs=[
            pl.BlockSpec((None, seq, gdim * groups), lambda b, l: (b, 0, l // nh),
                         pipeline_mode=pl.Buffered(cfg.h_buffers)),
            pl.BlockSpec((None, gdim, 2 * cb), lambda b, l: (l % nh, 0, 0)),
            pl.BlockSpec(f1.shape, lambda b, l: (0, 0), pipeline_mode=pl.Buffered(1)),
            pl.BlockSpec(h2.shape, lambda b, l: (0, 0, 0), pipeline_mode=pl.Buffered(1)),
        ],
        out_specs=pl.BlockSpec((None, n2 * po, cb * groups), lambda b, l: (b, 0, l)),
        scratch_shapes=[pltpu.VMEM((groups, 2, cb // LANES, n1 * pa, LANES), F32)]
        + [pltpu.VMEM((groups, 2, cb // LANES, seq, LANES), F32)] * split_mid,
        compiler_params=_params("parallel", "arbitrary"),
        name="fourier",
    )(h.reshape(batch, seq, d), fc, f1, h2)
    return out.reshape(batch * n2 * po, d)


def _proj_body(n1, po, x_ref, m_ref, w_ref, o_ref):
    if po == n1:
        m = m_ref[...]
    else:
        m = jnp.concatenate([m_ref[q * po:q * po + n1, :]
                             for q in range(m_ref.shape[0] // po)], axis=0)
    o_ref[...] = x_ref[...] + jnp.dot(m.astype(BF16), w_ref[...],
                                      preferred_element_type=F32)


def _proj_residual(x, mixed, cfg, w, layer, *, tm=TOKEN_TILE):
    t, d = x.shape
    mrows = tm // cfg.n1 * cfg.pitch_o
    return pl.pallas_call(
        functools.partial(_proj_body, cfg.n1, cfg.pitch_o),
        out_shape=jax.ShapeDtypeStruct((t, d), F32),
        grid=(t // tm,),
        in_specs=[pl.BlockSpec((tm, d), lambda i: (i, 0)),
                  pl.BlockSpec((mrows, d), lambda i: (i, 0)),
                  pl.BlockSpec((None, d, d), lambda i: (layer, 0, 0),
                               pipeline_mode=pl.Buffered(1))],
        out_specs=pl.BlockSpec((tm, d), lambda i: (i, 0)),
        compiler_params=_params("parallel"),
        name="proj_residual",
    )(x, mixed, w)


_FFT_CONFIG = {
    16384: FftCfg(seq=16384, n1=128, n2=128, kb=1, cb=128, pitch_a=132,
                  pitch_o=132, unroll=32, h_buffers=2, groups=1, split_mid=False),
    2048: FftCfg(seq=2048, n1=128, n2=16, kb=8, cb=256, pitch_a=20,
                 pitch_o=128, unroll=16, h_buffers=2, groups=2, split_mid=True),
}


def _trunk(x3, w):
    batch, seq, d = x3.shape
    x = x3.reshape(batch * seq, d)
    depth = w["norm_ffn"].shape[0]
    h = None
    for i in range(depth):
        j = i // 2
        if i % 2 == 0:
            x = _conv_mixer(x, seq, w["norm_mix"], w["a_w_in"], w["a_conv_w"],
                            w["a_w_out"], j)
        else:
            cfg = _FFT_CONFIG[seq]
            mixed = _fourier(h, batch, cfg)
            x = _proj_residual(x, mixed, cfg, w["f_w_out"], j)
        mlp = functools.partial(_mlp, x, w["norm_ffn"], w["w_up"], w["w_down"], i)
        if i == depth - 1:
            x = mlp("only", w["final_norm"], 0)
        elif (i + 1) % 2 == 1:
            x, h = mlp("also", w["norm_mix"], i + 1)
        else:
            x = mlp()
    return x.reshape(batch, seq, d)


def kernel(x_prompt, x_sample, norm_mix, a_w_in, a_conv_w, a_w_out, f_w_out,
           norm_ffn, w_up, w_down, final_norm):
    d = x_prompt.shape[-1]
    w = dict(
        norm_mix=norm_mix.reshape(-1, 1, d),
        norm_ffn=norm_ffn.reshape(-1, 1, d),
        final_norm=final_norm.reshape(1, 1, d),
        a_w_in=a_w_in.astype(BF16),
        a_conv_w=a_conv_w,
        a_w_out=a_w_out.astype(BF16),
        f_w_out=f_w_out.astype(BF16),
        w_up=w_up.astype(BF16),
        w_down=w_down.astype(BF16),
    )
    return _trunk(x_prompt, w), _trunk(x_sample, w)
```

```python
import functools
from typing import NamedTuple

import jax
import jax.numpy as jnp
import numpy as np
from jax import lax
from jax.experimental import pallas as pl
from jax.experimental.pallas import tpu as pltpu

F32 = jnp.float32
BF16 = jnp.bfloat16

RMS_EPS = 1e-6
N_FGROUPS = 8
LANES = 128
VMEM_BYTES_V7X = 64 * 1024 * 1024
VMEM_LIMIT_BYTES = VMEM_BYTES_V7X - 4 * 1024 * 1024

TOKEN_TILE = 512
MLP_FF_TILE = 2048
CONV_CH_TILE = 1024


def _params(*sem):
    return pltpu.CompilerParams(dimension_semantics=sem,
                                vmem_limit_bytes=VMEM_LIMIT_BYTES)


def _rms(x, g):
    ms = jnp.mean(x * x, axis=-1, keepdims=True)
    return (x * lax.rsqrt(ms + RMS_EPS)) * g


def _mlp_body(post, x_ref, g_ref, wu_ref, wd_ref, *rest):
    if post is None:
        acc_ref, h_ref = rest
    elif post == "also":
        g2_ref, acc_ref, n_ref, h_ref = rest
    else:
        g2_ref, n_ref, h_ref, acc_ref = rest
    j = pl.program_id(1)
    last_j = pl.num_programs(1) - 1

    def step(first, last):
        if first:
            x = x_ref[...]
            h = _rms(x, g_ref[...]).astype(BF16)
            h_ref[...] = h
        else:
            h = h_ref[...]
        u = jnp.dot(h, wu_ref[...], preferred_element_type=F32)
        u = jnp.maximum(u, 0.0)
        u = (u * u).astype(BF16)
        y = (x_ref[...] if first else acc_ref[...]) + jnp.dot(
            u, wd_ref[...], preferred_element_type=F32)
        if not (last and post == "only"):
            acc_ref[...] = y
        if last and post is not None:
            n_ref[...] = _rms(y, g2_ref[...]).astype(n_ref.dtype)

    pl.when(j == 0)(functools.partial(step, True, False))
    if post is None:
        pl.when(j > 0)(functools.partial(step, False, False))
    else:
        pl.when((j > 0) & (j < last_j))(functools.partial(step, False, False))
        pl.when(j == last_j)(functools.partial(step, False, True))


def _mlp(x, g, w_up, w_down, layer, post=None, g2=None, layer2=0, *, tm=TOKEN_TILE,
         tf=MLP_FF_TILE):
    t, d = x.shape
    ff = w_up.shape[-1]
    assert ff // tf >= 2
    tile = pl.BlockSpec((tm, d), lambda i, j: (i, 0))
    in_specs = [
        tile,
        pl.BlockSpec((None, 1, d), lambda i, j: (layer, 0, 0)),
        pl.BlockSpec((None, d, tf), lambda i, j: (layer, 0, j)),
        pl.BlockSpec((None, tf, d), lambda i, j: (layer, j, 0)),
    ]
    args = [x, g, w_up, w_down]
    scratch = [pltpu.VMEM((tm, d), BF16)]
    x_new = jax.ShapeDtypeStruct((t, d), F32)
    if post is None:
        out_shape, out_specs = x_new, tile
    else:
        in_specs.append(pl.BlockSpec((None, 1, d), lambda i, j: (layer2, 0, 0)))
        args.append(g2)
        if post == "also":
            out_shape = (x_new, jax.ShapeDtypeStruct((t, d), BF16))
            out_specs = (tile, tile)
        else:
            out_shape, out_specs = x_new, tile
            scratch.append(pltpu.VMEM((tm, d), F32))
    return pl.pallas_call(
        functools.partial(_mlp_body, post),
        out_shape=out_shape,
        grid=(t // tm, ff // tf),
        in_specs=in_specs,
        out_specs=out_specs,
        scratch_shapes=scratch,
        compiler_params=_params("parallel", "arbitrary"),
        name="mlp",
    )(*args)


HALO = 8


def _conv_body(seq, tm, x_ref, xp_ref, xn_ref, g_ref, wb_ref, wc_ref, wv_ref,
               cw_ref, wo_ref, o_ref, h_ref):
    i = pl.program_id(0)
    j = pl.program_id(1)
    rows = tm + 2 * HALO

    def step(first):
        if first:
            g = g_ref[...]
            keep_prev = ((i * tm) % seq != 0).astype(F32)
            keep_next = (((i + 1) * tm) % seq != 0).astype(F32)
            h = jnp.concatenate(
                [_rms(x_ref[...], g),
                 _rms(xn_ref[...], g) * keep_next,
                 _rms(xp_ref[...], g) * keep_prev], axis=0).astype(BF16)
            h_ref[...] = h
        else:
            h = h_ref[...]
        c = jnp.dot(h, wc_ref[...], preferred_element_type=F32)
        v = jnp.dot(h, wv_ref[...], preferred_element_type=F32)
        b = jnp.dot(h[:tm], wb_ref[...], preferred_element_type=F32)
        u = c * v
        cw = cw_ref[...]
        u_prev = pltpu.roll(u, 1, axis=0)[:tm]
        u_next = pltpu.roll(u, rows - 1, axis=0)[:tm]
        conv = cw[0:1] * u_prev + cw[1:2] * u[:tm] + cw[2:3] * u_next
        gated = (b * conv).astype(BF16)
        o_ref[...] = (x_ref[...] if first else o_ref[...]) + jnp.dot(
            gated, wo_ref[...], preferred_element_type=F32)

    pl.when(j == 0)(functools.partial(step, True))
    pl.when(j > 0)(functools.partial(step, False))


def _conv_mixer(x, seq, g, w_in, conv_w, w_out, layer, *, tm=TOKEN_TILE,
                tc=CONV_CH_TILE):
    t, d = x.shape
    cdim = conv_w.shape[-1]
    nc = cdim // tc
    hb = tm // HALO
    last = t // HALO - 1
    return pl.pallas_call(
        functools.partial(_conv_body, seq, tm),
        out_shape=jax.ShapeDtypeStruct((t, d), F32),
        grid=(t // tm, nc),
        in_specs=[
            pl.BlockSpec((tm, d), lambda i, j: (i, 0)),
            pl.BlockSpec((HALO, d), lambda i, j: (jnp.maximum(i * hb - 1, 0), 0)),
            pl.BlockSpec((HALO, d), lambda i, j: (jnp.minimum((i + 1) * hb, last), 0)),
            pl.BlockSpec((None, 1, d), lambda i, j: (2 * layer, 0, 0)),
            pl.BlockSpec((None, d, tc), lambda i, j: (layer, 0, j)),
            pl.BlockSpec((None, d, tc), lambda i, j: (layer, 0, nc + j)),
            pl.BlockSpec((None, d, tc), lambda i, j: (layer, 0, 2 * nc + j)),
            pl.BlockSpec((None, 3, tc), lambda i, j: (layer, 0, j)),
            pl.BlockSpec((None, tc, d), lambda i, j: (layer, j, 0)),
        ],
        out_specs=pl.BlockSpec((tm, d), lambda i, j: (i, 0)),
        scratch_shapes=[pltpu.VMEM((tm + 2 * HALO, d), BF16)],
        compiler_params=_params("parallel", "arbitrary"),
        name="conv_mixer",
    )(x, x, x, g, w_in, w_in, w_in, conv_w, w_out)


def _dft_tables(seq, n1, n2, kb, cb, gdim, split_mid):
    c = np.arange(gdim)
    ang = 2.0 * np.pi * ((c[:, None] * c[None, :]) % gdim) / gdim
    fre, fim = np.cos(ang) / np.sqrt(gdim), -np.sin(ang) / np.sqrt(gdim)
    nh = gdim // cb
    fc = np.stack([np.concatenate([fre[:, q * cb:(q + 1) * cb],
                                   fim[:, q * cb:(q + 1) * cb]], axis=1)
                   for q in range(nh)])

    k = np.arange(n1)
    ang = 2.0 * np.pi * ((k[:, None] * k[None, :]) % n1) / n1
    gr, gi = np.cos(ang) / np.sqrt(n1), -np.sin(ang) / np.sqrt(n1)
    f1 = np.block([[gr, -gi], [gi, gr]])

    na = n1 // kb
    m = kb * n2
    h2 = np.zeros((na, m, 2 * m))
    nn = np.arange(n2)
    for a in range(na):
        for i in range(kb):
            k1 = a * kb + i
            kk = k1 + n1 * np.arange(n2)
            ang = 2.0 * np.pi * ((kk[:, None] * nn[None, :]) % seq) / seq
            wr, wi = np.cos(ang) / np.sqrt(n2), -np.sin(ang) / np.sqrt(n2)
            rows = np.arange(n2) * kb + i
            cols = nn * kb + i if split_mid else i * n2 + nn
            h2[a, rows[:, None], cols[None, :]] = wr
            h2[a, rows[:, None], m + cols[None, :]] = -wi
    return (np.asarray(fc, np.float32), np.asarray(f1, np.float32),
            np.asarray(h2, np.float32))


class FftCfg(NamedTuple):
    seq: int
    n1: int
    n2: int
    kb: int
    cb: int
    pitch_a: int
    pitch_o: int
    unroll: int
    h_buffers: int
    groups: int
    split_mid: bool


class _Idx(NamedTuple):
    blk: object
    span: int
    u: int

    def value(self):
        return self.blk * self.span + self.u

    def times(self, m):
        step = self.span * m
        base = self.blk * step
        if not isinstance(base, int):
            base = pl.multiple_of(base, min(step & -step, 1024))
        return base + self.u * m


def _loop(n, body, unroll):
    if unroll >= n:
        for u in range(n):
            body(_Idx(0, n, u))
    else:
        def trip(blk, carry):
            for u in range(unroll):
                body(_Idx(blk, unroll, u))
            return carry
        lax.fori_loop(0, n // unroll, trip, 0)


def _fft_body(cfg, h_ref, fc_ref, f1_ref, h2_ref, o_ref, a_ref, *b_ref):
    seq, n1, n2, kb, cb, pa, po, unroll, _, groups, split_mid = cfg
    mid_ref = b_ref[0] if split_mid else a_ref
    nl = cb // LANES
    gdim = h_ref.shape[1] // groups
    rc = max(n2, min(seq, 512))
    per = rc // n2

    def chan(r):
        for g in range(groups):
            w = jnp.dot(h_ref[pl.ds(r.times(rc), rc), g * gdim:(g + 1) * gdim],
                        fc_ref[...], preferred_element_type=F32)
            for q in range(per if pa != n2 else 1):
                size = n2 if pa != n2 else rc
                dst = pl.ds(r.times(per * pa) + q * pa, size)
                for p in range(2):
                    for s in range(nl):
                        lo = p * cb + s * LANES
                        a_ref[g, p, s, dst, :] = w[q * n2:q * n2 + size, lo:lo + LANES]
    _loop(seq // rc, chan, unroll)

    def load(ref, g, rows):
        return jnp.concatenate(
            [jnp.concatenate(
                [jnp.concatenate([ref[g, p, s, r, :] for r in rows], axis=0)
                 for s in range(nl)], axis=1)
             for p in range(2)], axis=0).astype(BF16)

    nb = max(1, 2 * LANES // cb)
    def stage1(t):
        rows = [pl.ds(t.value() * nb + q, n1, stride=pa) for q in range(nb)]
        for g in range(groups):
            rhs = jnp.concatenate([load(a_ref, g, [r]) for r in rows], axis=1)
            y = jnp.dot(f1_ref[...], rhs, preferred_element_type=F32)
            for q in range(nb):
                dst = (pl.ds(t.times(nb * n1) + q * n1, n1) if split_mid
                       else rows[q])
                for p in range(2):
                    for s in range(nl):
                        lo = q * cb + s * LANES
                        mid_ref[g, p, s, dst, :] = y[p * n1:(p + 1) * n1, lo:lo + LANES]
    _loop(n2 // nb, stage1, unroll)

    m = kb * n2
    def stage2(a):
        for g in range(groups):
            if split_mid:
                src = [pl.ds(i2 * n1 + a.times(kb), kb) for i2 in range(n2)]
            else:
                src = [pl.ds(a.times(kb * pa), m)]
            y = jnp.dot(h2_ref[a.value()], load(mid_ref, g, src),
                        preferred_element_type=F32)
            cols = slice(g * cb, (g + 1) * cb)
            if kb == 1:
                o_ref[pl.ds(a.value(), n2, stride=po), cols] = y
            else:
                for k2 in range(n2):
                    o_ref[pl.ds(a.times(kb) + k2 * po, kb), cols] = (
                        y[k2 * kb:(k2 + 1) * kb])
    _loop(n1 // kb, stage2, unroll)

    if po != n1:
        for k2 in range(n2):
            o_ref[k2 * po + n1:(k2 + 1) * po, :] = jnp.zeros(
                (po - n1, groups * cb), F32)


def _fourier(h, batch, cfg):
    t, d = h.shape
    seq, n1, n2, kb, cb, pa, po, _, _, groups, split_mid = cfg
    gdim = d // N_FGROUPS
    nh = gdim // cb
    assert kb == 1 or pa == n2 or split_mid
    assert groups == 1 or nh == 1
    assert not split_mid or kb % 8 == 0
    fc, f1, h2 = (jnp.asarray(tab).astype(BF16)
                  for tab in _dft_tables(seq, n1, n2, kb, cb, gdim, split_mid))
    out = pl.pallas_call(
        functools.partial(_fft_body, cfg),
        out_shape=jax.ShapeDtypeStruct((batch, n2 * po, d), F32),
        grid=(batch, d // (cb * groups)),
        in_specs=[
            pl.BlockSpec((None, seq, gdim * groups), lambda b, l: (b, 0, l // nh),
                         pipeline_mode=pl.Buffered(cfg.h_buffers)),
            pl.BlockSpec((None, gdim, 2 * cb), lambda b, l: (l % nh, 0, 0)),
            pl.BlockSpec(f1.shape, lambda b, l: (0, 0), pipeline_mode=pl.Buffered(1)),
            pl.BlockSpec(h2.shape, lambda b, l: (0, 0, 0), pipeline_mode=pl.Buffered(1)),
        ],
        out_specs=pl.BlockSpec((None, n2 * po, cb * groups), lambda b, l: (b, 0, l)),
        scratch_shapes=[pltpu.VMEM((groups, 2, cb // LANES, n1 * pa, LANES), F32)]
        + [pltpu.VMEM((groups, 2, cb // LANES, seq, LANES), F32)] * split_mid,
        compiler_params=_params("parallel", "arbitrary"),
        name="fourier",
    )(h.reshape(batch, seq, d), fc, f1, h2)
    return out.reshape(batch * n2 * po, d)


def _proj_body(n1, po, x_ref, m_ref, w_ref, o_ref):
    if po == n1:
        m = m_ref[...]
    else:
        m = jnp.concatenate([m_ref[q * po:q * po + n1, :]
                             for q in range(m_ref.shape[0] // po)], axis=0)
    o_ref[...] = x_ref[...] + jnp.dot(m.astype(BF16), w_ref[...],
                                      preferred_element_type=F32)


def _proj_residual(x, mixed, cfg, w, layer, *, tm=TOKEN_TILE):
    t, d = x.shape
    mrows = tm // cfg.n1 * cfg.pitch_o
    return pl.pallas_call(
        functools.partial(_proj_body, cfg.n1, cfg.pitch_o),
        out_shape=jax.ShapeDtypeStruct((t, d), F32),
        grid=(t // tm,),
        in_specs=[pl.BlockSpec((tm, d), lambda i: (i, 0)),
                  pl.BlockSpec((mrows, d), lambda i: (i, 0)),
                  pl.BlockSpec((None, d, d), lambda i: (layer, 0, 0),
                               pipeline_mode=pl.Buffered(1))],
        out_specs=pl.BlockSpec((tm, d), lambda i: (i, 0)),
        compiler_params=_params("parallel"),
        name="proj_residual",
    )(x, mixed, w)


_FFT_CONFIG = {
    16384: FftCfg(seq=16384, n1=128, n2=128, kb=1, cb=128, pitch_a=132,
                  pitch_o=132, unroll=128, h_buffers=2, groups=1, split_mid=False),
    2048: FftCfg(seq=2048, n1=128, n2=16, kb=8, cb=256, pitch_a=20,
                 pitch_o=128, unroll=16, h_buffers=2, groups=2, split_mid=True),
}


def _trunk(x3, w):
    batch, seq, d = x3.shape
    x = x3.reshape(batch * seq, d)
    depth = w["norm_ffn"].shape[0]
    h = None
    for i in range(depth):
        j = i // 2
        if i % 2 == 0:
            x = _conv_mixer(x, seq, w["norm_mix"], w["a_w_in"], w["a_conv_w"],
                            w["a_w_out"], j)
        else:
            cfg = _FFT_CONFIG[seq]
            mixed = _fourier(h, batch, cfg)
            x = _proj_residual(x, mixed, cfg, w["f_w_out"], j)
        mlp = functools.partial(_mlp, x, w["norm_ffn"], w["w_up"], w["w_down"], i)
        if i == depth - 1:
            x = mlp("only", w["final_norm"], 0)
        elif (i + 1) % 2 == 1:
            x, h = mlp("also", w["norm_mix"], i + 1)
        else:
            x = mlp()
    return x.reshape(batch, seq, d)


def kernel(x_prompt, x_sample, norm_mix, a_w_in, a_conv_w, a_w_out, f_w_out,
           norm_ffn, w_up, w_down, final_norm):
    d = x_prompt.shape[-1]
    w = dict(
        norm_mix=norm_mix.reshape(-1, 1, d),
        norm_ffn=norm_ffn.reshape(-1, 1, d),
        final_norm=final_norm.reshape(1, 1, d),
        a_w_in=a_w_in.astype(BF16),
        a_conv_w=a_conv_w,
        a_w_out=a_w_out.astype(BF16),
        f_w_out=f_w_out.astype(BF16),
        w_up=w_up.astype(BF16),
        w_down=w_down.astype(BF16),
    )
    return _trunk(x_prompt, w), _trunk(x_sample, w)
```

```python
import functools
from typing import NamedTuple

import jax
import jax.numpy as jnp
import numpy as np
from jax import lax
from jax.experimental import pallas as pl
from jax.experimental.pallas import tpu as pltpu

F32 = jnp.float32
BF16 = jnp.bfloat16

RMS_EPS = 1e-6
N_FGROUPS = 8
LANES = 128
VMEM_BYTES_V7X = 64 * 1024 * 1024
VMEM_LIMIT_BYTES = VMEM_BYTES_V7X - 4 * 1024 * 1024

TOKEN_TILE = 512
MLP_FF_TILE = 2048
CONV_CH_TILE = 1024


def _params(*sem):
    return pltpu.CompilerParams(dimension_semantics=sem,
                                vmem_limit_bytes=VMEM_LIMIT_BYTES)


def _rms(x, g):
    ms = jnp.mean(x * x, axis=-1, keepdims=True)
    return (x * lax.rsqrt(ms + RMS_EPS)) * g


def _mlp_body(post, x_ref, g_ref, wu_ref, wd_ref, *rest):
    if post is None:
        acc_ref, h_ref = rest
    elif post == "also":
        g2_ref, acc_ref, n_ref, h_ref = rest
    else:
        g2_ref, n_ref, h_ref, acc_ref = rest
    j = pl.program_id(1)
    last_j = pl.num_programs(1) - 1

    def step(first, last):
        if first:
            x = x_ref[...]
            h = _rms(x, g_ref[...]).astype(BF16)
            h_ref[...] = h
        else:
            h = h_ref[...]
        u = jnp.dot(h, wu_ref[...], preferred_element_type=F32)
        u = jnp.maximum(u, 0.0)
        u = (u * u).astype(BF16)
        y = (x_ref[...] if first else acc_ref[...]) + jnp.dot(
            u, wd_ref[...], preferred_element_type=F32)
        if not (last and post == "only"):
            acc_ref[...] = y
        if last and post is not None:
            n_ref[...] = _rms(y, g2_ref[...]).astype(n_ref.dtype)

    pl.when(j == 0)(functools.partial(step, True, False))
    if post is None:
        pl.when(j > 0)(functools.partial(step, False, False))
    else:
        pl.when((j > 0) & (j < last_j))(functools.partial(step, False, False))
        pl.when(j == last_j)(functools.partial(step, False, True))


def _mlp(x, g, w_up, w_down, layer, post=None, g2=None, layer2=0, *, tm=TOKEN_TILE):
    t, d = x.shape
    nj, tf = w_up.shape[1], w_up.shape[3]
    assert nj >= 2
    tile = pl.BlockSpec((tm, d), lambda i, j: (i, 0))
    in_specs = [
        tile,
        pl.BlockSpec((None, 1, d), lambda i, j: (layer, 0, 0)),
        pl.BlockSpec((None, None, d, tf), lambda i, j: (layer, j, 0, 0)),
        pl.BlockSpec((None, tf, d), lambda i, j: (layer, j, 0)),
    ]
    args = [x, g, w_up, w_down]
    scratch = [pltpu.VMEM((tm, d), BF16)]
    x_new = jax.ShapeDtypeStruct((t, d), F32)
    if post is None:
        out_shape, out_specs = x_new, tile
    else:
        in_specs.append(pl.BlockSpec((None, 1, d), lambda i, j: (layer2, 0, 0)))
        args.append(g2)
        if post == "also":
            out_shape = (x_new, jax.ShapeDtypeStruct((t, d), BF16))
            out_specs = (tile, tile)
        else:
            out_shape, out_specs = x_new, tile
            scratch.append(pltpu.VMEM((tm, d), F32))
    return pl.pallas_call(
        functools.partial(_mlp_body, post),
        out_shape=out_shape,
        grid=(t // tm, nj),
        in_specs=in_specs,
        out_specs=out_specs,
        scratch_shapes=scratch,
        compiler_params=_params("parallel", "arbitrary"),
        name="mlp",
    )(*args)


HALO = 8


def _conv_body(seq, tm, x_ref, xp_ref, xn_ref, g_ref, wb_ref, wc_ref, wv_ref,
               cw_ref, wo_ref, o_ref, h_ref):
    i = pl.program_id(0)
    j = pl.program_id(1)
    rows = tm + 2 * HALO

    def step(first):
        if first:
            g = g_ref[...]
            keep_prev = ((i * tm) % seq != 0).astype(F32)
            keep_next = (((i + 1) * tm) % seq != 0).astype(F32)
            h = jnp.concatenate(
                [_rms(x_ref[...], g),
                 _rms(xn_ref[...], g) * keep_next,
                 _rms(xp_ref[...], g) * keep_prev], axis=0).astype(BF16)
            h_ref[...] = h
        else:
            h = h_ref[...]
        c = jnp.dot(h, wc_ref[...], preferred_element_type=F32)
        v = jnp.dot(h, wv_ref[...], preferred_element_type=F32)
        b = jnp.dot(h[:tm], wb_ref[...], preferred_element_type=F32)
        u = c * v
        cw = cw_ref[...]
        u_prev = pltpu.roll(u, 1, axis=0)[:tm]
        u_next = pltpu.roll(u, rows - 1, axis=0)[:tm]
        conv = cw[0:1] * u_prev + cw[1:2] * u[:tm] + cw[2:3] * u_next
        gated = (b * conv).astype(BF16)
        o_ref[...] = (x_ref[...] if first else o_ref[...]) + jnp.dot(
            gated, wo_ref[...], preferred_element_type=F32)

    pl.when(j == 0)(functools.partial(step, True))
    pl.when(j > 0)(functools.partial(step, False))


def _conv_mixer(x, seq, g, w_in, conv_w, w_out, layer, *, tm=TOKEN_TILE):
    t, d = x.shape
    tc = w_in.shape[3]
    nc = w_in.shape[1] // 3
    hb = tm // HALO
    last = t // HALO - 1
    return pl.pallas_call(
        functools.partial(_conv_body, seq, tm),
        out_shape=jax.ShapeDtypeStruct((t, d), F32),
        grid=(t // tm, nc),
        in_specs=[
            pl.BlockSpec((tm, d), lambda i, j: (i, 0)),
            pl.BlockSpec((HALO, d), lambda i, j: (jnp.maximum(i * hb - 1, 0), 0)),
            pl.BlockSpec((HALO, d), lambda i, j: (jnp.minimum((i + 1) * hb, last), 0)),
            pl.BlockSpec((None, 1, d), lambda i, j: (2 * layer, 0, 0)),
            pl.BlockSpec((None, None, d, tc), lambda i, j: (layer, j, 0, 0)),
            pl.BlockSpec((None, None, d, tc), lambda i, j: (layer, nc + j, 0, 0)),
            pl.BlockSpec((None, None, d, tc), lambda i, j: (layer, 2 * nc + j, 0, 0)),
            pl.BlockSpec((None, 3, tc), lambda i, j: (layer, 0, j)),
            pl.BlockSpec((None, tc, d), lambda i, j: (layer, j, 0)),
        ],
        out_specs=pl.BlockSpec((tm, d), lambda i, j: (i, 0)),
        scratch_shapes=[pltpu.VMEM((tm + 2 * HALO, d), BF16)],
        compiler_params=_params("parallel", "arbitrary"),
        name="conv_mixer",
    )(x, x, x, g, w_in, w_in, w_in, conv_w, w_out)


def _dft_tables(seq, n1, n2, kb, cb, gdim, split_mid):
    c = np.arange(gdim)
    ang = 2.0 * np.pi * ((c[:, None] * c[None, :]) % gdim) / gdim
    fre, fim = np.cos(ang) / np.sqrt(gdim), -np.sin(ang) / np.sqrt(gdim)
    nh = gdim // cb
    fc = np.stack([np.concatenate([fre[:, q * cb:(q + 1) * cb],
                                   fim[:, q * cb:(q + 1) * cb]], axis=1)
                   for q in range(nh)])

    k = np.arange(n1)
    ang = 2.0 * np.pi * ((k[:, None] * k[None, :]) % n1) / n1
    gr, gi = np.cos(ang) / np.sqrt(n1), -np.sin(ang) / np.sqrt(n1)
    f1 = np.block([[gr, -gi], [gi, gr]])

    na = n1 // kb
    m = kb * n2
    h2 = np.zeros((na, m, 2 * m))
    nn = np.arange(n2)
    for a in range(na):
        for i in range(kb):
            k1 = a * kb + i
            kk = k1 + n1 * np.arange(n2)
            ang = 2.0 * np.pi * ((kk[:, None] * nn[None, :]) % seq) / seq
            wr, wi = np.cos(ang) / np.sqrt(n2), -np.sin(ang) / np.sqrt(n2)
            rows = np.arange(n2) * kb + i
            cols = nn * kb + i if split_mid else i * n2 + nn
            h2[a, rows[:, None], cols[None, :]] = wr
            h2[a, rows[:, None], m + cols[None, :]] = -wi
    return (np.asarray(fc, np.float32), np.asarray(f1, np.float32),
            np.asarray(h2, np.float32))


class FftCfg(NamedTuple):
    seq: int
    n1: int
    n2: int
    kb: int
    cb: int
    pitch_a: int
    pitch_o: int
    unroll: int
    h_buffers: int
    groups: int
    split_mid: bool


class _Idx(NamedTuple):
    blk: object
    span: int
    u: int

    def value(self):
        return self.blk * self.span + self.u

    def times(self, m):
        step = self.span * m
        base = self.blk * step
        if not isinstance(base, int):
            base = pl.multiple_of(base, min(step & -step, 1024))
        return base + self.u * m


def _loop(n, body, unroll):
    if unroll >= n:
        for u in range(n):
            body(_Idx(0, n, u))
    else:
        def trip(blk, carry):
            for u in range(unroll):
                body(_Idx(blk, unroll, u))
            return carry
        lax.fori_loop(0, n // unroll, trip, 0)


def _fft_body(cfg, h_ref, fc_ref, f1_ref, h2_ref, o_ref, a_ref, *b_ref):
    seq, n1, n2, kb, cb, pa, po, unroll, _, groups, split_mid = cfg
    mid_ref = b_ref[0] if split_mid else a_ref
    nl = cb // LANES
    gdim = h_ref.shape[1] // groups
    rc = max(n2, min(seq, 512))
    per = rc // n2

    def chan(r):
        for g in range(groups):
            w = jnp.dot(h_ref[pl.ds(r.times(rc), rc), g * gdim:(g + 1) * gdim],
                        fc_ref[...], preferred_element_type=F32)
            for q in range(per if pa != n2 else 1):
                size = n2 if pa != n2 else rc
                dst = pl.ds(r.times(per * pa) + q * pa, size)
                for p in range(2):
                    for s in range(nl):
                        lo = p * cb + s * LANES
                        a_ref[g, p, s, dst, :] = w[q * n2:q * n2 + size, lo:lo + LANES]
    _loop(seq // rc, chan, unroll)

    def load(ref, g, rows):
        return jnp.concatenate(
            [jnp.concatenate(
                [jnp.concatenate([ref[g, p, s, r, :] for r in rows], axis=0)
                 for s in range(nl)], axis=1)
             for p in range(2)], axis=0).astype(BF16)

    nb = max(1, 2 * LANES // cb)
    def stage1(t):
        rows = [pl.ds(t.value() * nb + q, n1, stride=pa) for q in range(nb)]
        for g in range(groups):
            rhs = jnp.concatenate([load(a_ref, g, [r]) for r in rows], axis=1)
            y = jnp.dot(f1_ref[...], rhs, preferred_element_type=F32)
            for q in range(nb):
                dst = (pl.ds(t.times(nb * n1) + q * n1, n1) if split_mid
                       else rows[q])
                for p in range(2):
                    for s in range(nl):
                        lo = q * cb + s * LANES
                        mid_ref[g, p, s, dst, :] = y[p * n1:(p + 1) * n1, lo:lo + LANES]
    _loop(n2 // nb, stage1, unroll)

    m = kb * n2
    def stage2(a):
        for g in range(groups):
            if split_mid:
                src = [pl.ds(i2 * n1 + a.times(kb), kb) for i2 in range(n2)]
            else:
                src = [pl.ds(a.times(kb * pa), m)]
            y = jnp.dot(h2_ref[a.value()], load(mid_ref, g, src),
                        preferred_element_type=F32)
            cols = slice(g * cb, (g + 1) * cb)
            if kb == 1:
                o_ref[pl.ds(a.value(), n2, stride=po), cols] = y
            else:
                for k2 in range(n2):
                    o_ref[pl.ds(a.times(kb) + k2 * po, kb), cols] = (
                        y[k2 * kb:(k2 + 1) * kb])
    _loop(n1 // kb, stage2, unroll)

    if po != n1:
        for k2 in range(n2):
            o_ref[k2 * po + n1:(k2 + 1) * po, :] = jnp.zeros(
                (po - n1, groups * cb), F32)


def _fourier(h, batch, cfg):
    t, d = h.shape
    seq, n1, n2, kb, cb, pa, po, _, _, groups, split_mid = cfg
    gdim = d // N_FGROUPS
    nh = gdim // cb
    assert kb == 1 or pa == n2 or split_mid
    assert groups == 1 or nh == 1
    assert not split_mid or kb % 8 == 0
    fc, f1, h2 = (jnp.asarray(tab).astype(BF16)
                  for tab in _dft_tables(seq, n1, n2, kb, cb, gdim, split_mid))
    out = pl.pallas_call(
        functools.partial(_fft_body, cfg),
        out_shape=jax.ShapeDtypeStruct((batch, n2 * po, d), F32),
        grid=(batch, d // (cb * groups)),
        in_specs=[
            pl.BlockSpec((None, seq, gdim * groups), lambda b, l: (b, 0, l // nh),
                         pipeline_mode=pl.Buffered(cfg.h_buffers)),
            pl.BlockSpec((None, gdim, 2 * cb), lambda b, l: (l % nh, 0, 0)),
            pl.BlockSpec(f1.shape, lambda b, l: (0, 0), pipeline_mode=pl.Buffered(1)),
            pl.BlockSpec(h2.shape, lambda b, l: (0, 0, 0), pipeline_mode=pl.Buffered(1)),
        ],
        out_specs=pl.BlockSpec((None, n2 * po, cb * groups), lambda b, l: (b, 0, l)),
        scratch_shapes=[pltpu.VMEM((groups, 2, cb // LANES, n1 * pa, LANES), F32)]
        + [pltpu.VMEM((groups, 2, cb // LANES, seq, LANES), F32)] * split_mid,
        compiler_params=_params("parallel", "arbitrary"),
        name="fourier",
    )(h.reshape(batch, seq, d), fc, f1, h2)
    return out.reshape(batch * n2 * po, d)


def _proj_body(n1, po, x_ref, m_ref, w_ref, o_ref):
    if po == n1:
        m = m_ref[...]
    else:
        m = jnp.concatenate([m_ref[q * po:q * po + n1, :]
                             for q in range(m_ref.shape[0] // po)], axis=0)
    o_ref[...] = x_ref[...] + jnp.dot(m.astype(BF16), w_ref[...],
                                      preferred_element_type=F32)


def _proj_residual(x, mixed, cfg, w, layer, *, tm=TOKEN_TILE):
    t, d = x.shape
    mrows = tm // cfg.n1 * cfg.pitch_o
    return pl.pallas_call(
        functools.partial(_proj_body, cfg.n1, cfg.pitch_o),
        out_shape=jax.ShapeDtypeStruct((t, d), F32),
        grid=(t // tm,),
        in_specs=[pl.BlockSpec((tm, d), lambda i: (i, 0)),
                  pl.BlockSpec((mrows, d), lambda i: (i, 0)),
                  pl.BlockSpec((None, d, d), lambda i: (layer, 0, 0),
                               pipeline_mode=pl.Buffered(1))],
        out_specs=pl.BlockSpec((tm, d), lambda i: (i, 0)),
        compiler_params=_params("parallel"),
        name="proj_residual",
    )(x, mixed, w)


_FFT_CONFIG = {
    16384: FftCfg(seq=16384, n1=128, n2=128, kb=1, cb=128, pitch_a=132,
                  pitch_o=132, unroll=128, h_buffers=2, groups=1, split_mid=False),
    2048: FftCfg(seq=2048, n1=128, n2=16, kb=8, cb=256, pitch_a=20,
                 pitch_o=128, unroll=16, h_buffers=2, groups=2, split_mid=True),
}


def _trunk(x3, w):
    batch, seq, d = x3.shape
    x = x3.reshape(batch * seq, d)
    depth = w["norm_ffn"].shape[0]
    h = None
    for i in range(depth):
        j = i // 2
        if i % 2 == 0:
            x = _conv_mixer(x, seq, w["norm_mix"], w["a_w_in"], w["a_conv_w"],
                            w["a_w_out"], j)
        else:
            cfg = _FFT_CONFIG[seq]
            mixed = _fourier(h, batch, cfg)
            x = _proj_residual(x, mixed, cfg, w["f_w_out"], j)
        mlp = functools.partial(_mlp, x, w["norm_ffn"], w["w_up"], w["w_down"], i)
        if i == depth - 1:
            x = mlp("only", w["final_norm"], 0)
        elif (i + 1) % 2 == 1:
            x, h = mlp("also", w["norm_mix"], i + 1)
        else:
            x = mlp()
    return x.reshape(batch, seq, d)


def _chunk_major(w, tile):
    layers, d, n = w.shape
    return w.astype(BF16).reshape(layers, d, n // tile, tile).transpose(0, 2, 1, 3)


def kernel(x_prompt, x_sample, norm_mix, a_w_in, a_conv_w, a_w_out, f_w_out,
           norm_ffn, w_up, w_down, final_norm):
    d = x_prompt.shape[-1]
    w = dict(
        norm_mix=norm_mix.reshape(-1, 1, d),
        norm_ffn=norm_ffn.reshape(-1, 1, d),
        final_norm=final_norm.reshape(1, 1, d),
        a_w_in=_chunk_major(a_w_in, CONV_CH_TILE),
        a_conv_w=a_conv_w,
        a_w_out=a_w_out.astype(BF16),
        f_w_out=f_w_out.astype(BF16),
        w_up=_chunk_major(w_up, MLP_FF_TILE),
        w_down=w_down.astype(BF16),
    )
    return _trunk(x_prompt, w), _trunk(x_sample, w)
```

```python
import functools
from typing import NamedTuple

import jax
import jax.numpy as jnp
import numpy as np
from jax import lax
from jax.experimental import pallas as pl
from jax.experimental.pallas import tpu as pltpu

F32 = jnp.float32
BF16 = jnp.bfloat16

RMS_EPS = 1e-6
N_FGROUPS = 8
LANES = 128
VMEM_BYTES_V7X = 64 * 1024 * 1024
VMEM_LIMIT_BYTES = VMEM_BYTES_V7X - 2 * 1024 * 1024

TOKEN_TILE = 512
MLP_FF_TILE = 2048
WIDE_TOKEN_TILE = 1024
CONV_CH_TILE = 512


def _params(*sem):
    return pltpu.CompilerParams(dimension_semantics=sem,
                                vmem_limit_bytes=VMEM_LIMIT_BYTES)


def _rms(x, g):
    ms = jnp.mean(x * x, axis=-1, keepdims=True)
    return (x * lax.rsqrt(ms + RMS_EPS)) * g


def _mlp_body(post, x_ref, g_ref, wu_ref, wd_ref, *rest):
    if post is None:
        acc_ref, h_ref = rest
    elif post == "also":
        g2_ref, acc_ref, n_ref, h_ref = rest
    else:
        g2_ref, n_ref, h_ref, acc_ref = rest
    j = pl.program_id(1)
    last_j = pl.num_programs(1) - 1

    def step(first, last):
        if first:
            x = x_ref[...]
            h = _rms(x, g_ref[...]).astype(BF16)
            h_ref[...] = h
        else:
            h = h_ref[...]
        u = jnp.dot(h, wu_ref[...], preferred_element_type=F32)
        u = jnp.maximum(u, 0.0)
        u = (u * u).astype(BF16)
        y = (x_ref[...] if first else acc_ref[...]) + jnp.dot(
            u, wd_ref[...], preferred_element_type=F32)
        if not (last and post == "only"):
            acc_ref[...] = y
        if last and post is not None:
            n_ref[...] = _rms(y, g2_ref[...]).astype(n_ref.dtype)

    pl.when(j == 0)(functools.partial(step, True, False))
    if post is None:
        pl.when(j > 0)(functools.partial(step, False, False))
    else:
        pl.when((j > 0) & (j < last_j))(functools.partial(step, False, False))
        pl.when(j == last_j)(functools.partial(step, False, True))


def _mlp(x, g, w_up, w_down, layer, post=None, g2=None, layer2=0, *, tm=TOKEN_TILE,
         tf=MLP_FF_TILE):
    t, d = x.shape
    ff = w_up.shape[-1]
    assert ff // tf >= 2
    tile = pl.BlockSpec((tm, d), lambda i, j: (i, 0))
    in_specs = [
        tile,
        pl.BlockSpec((None, 1, d), lambda i, j: (layer, 0, 0)),
        pl.BlockSpec((None, d, tf), lambda i, j: (layer, 0, j)),
        pl.BlockSpec((None, tf, d), lambda i, j: (layer, j, 0)),
    ]
    args = [x, g, w_up, w_down]
    scratch = [pltpu.VMEM((tm, d), BF16)]
    x_new = jax.ShapeDtypeStruct((t, d), F32)
    if post is None:
        out_shape, out_specs = x_new, tile
    else:
        in_specs.append(pl.BlockSpec((None, 1, d), lambda i, j: (layer2, 0, 0)))
        args.append(g2)
        if post == "also":
            out_shape = (x_new, jax.ShapeDtypeStruct((t, d), BF16))
            out_specs = (tile, tile)
        else:
            out_shape, out_specs = x_new, tile
            scratch.append(pltpu.VMEM((tm, d), F32))
    return pl.pallas_call(
        functools.partial(_mlp_body, post),
        out_shape=out_shape,
        grid=(t // tm, ff // tf),
        in_specs=in_specs,
        out_specs=out_specs,
        scratch_shapes=scratch,
        compiler_params=_params("parallel", "arbitrary"),
        name="mlp",
    )(*args)


HALO = 8


def _conv_body(seq, tm, x_ref, xp_ref, xn_ref, g_ref, wb_ref, wc_ref, wv_ref,
               cw_ref, wo_ref, o_ref, h_ref):
    i = pl.program_id(0)
    j = pl.program_id(1)
    rows = tm + 2 * HALO

    def step(first):
        if first:
            g = g_ref[...]
            keep_prev = ((i * tm) % seq != 0).astype(F32)
            keep_next = (((i + 1) * tm) % seq != 0).astype(F32)
            h = jnp.concatenate(
                [_rms(x_ref[...], g),
                 _rms(xn_ref[...], g) * keep_next,
                 _rms(xp_ref[...], g) * keep_prev], axis=0).astype(BF16)
            h_ref[...] = h
        else:
            h = h_ref[...]
        c = jnp.dot(h, wc_ref[...], preferred_element_type=F32)
        v = jnp.dot(h, wv_ref[...], preferred_element_type=F32)
        b = jnp.dot(h[:tm], wb_ref[...], preferred_element_type=F32)
        u = c * v
        cw = cw_ref[...]
        u_prev = pltpu.roll(u, 1, axis=0)[:tm]
        u_next = pltpu.roll(u, rows - 1, axis=0)[:tm]
        conv = cw[0:1] * u_prev + cw[1:2] * u[:tm] + cw[2:3] * u_next
        gated = (b * conv).astype(BF16)
        o_ref[...] = (x_ref[...] if first else o_ref[...]) + jnp.dot(
            gated, wo_ref[...], preferred_element_type=F32)

    pl.when(j == 0)(functools.partial(step, True))
    pl.when(j > 0)(functools.partial(step, False))


def _conv_mixer(x, seq, g, w_in, conv_w, w_out, layer, *, tm=WIDE_TOKEN_TILE,
                tc=CONV_CH_TILE):
    t, d = x.shape
    cdim = conv_w.shape[-1]
    nc = cdim // tc
    hb = tm // HALO
    last = t // HALO - 1
    return pl.pallas_call(
        functools.partial(_conv_body, seq, tm),
        out_shape=jax.ShapeDtypeStruct((t, d), F32),
        grid=(t // tm, nc),
        in_specs=[
            pl.BlockSpec((tm, d), lambda i, j: (i, 0)),
            pl.BlockSpec((HALO, d), lambda i, j: (jnp.maximum(i * hb - 1, 0), 0)),
            pl.BlockSpec((HALO, d), lambda i, j: (jnp.minimum((i + 1) * hb, last), 0)),
            pl.BlockSpec((None, 1, d), lambda i, j: (2 * layer, 0, 0)),
            pl.BlockSpec((None, d, tc), lambda i, j: (layer, 0, j)),
            pl.BlockSpec((None, d, tc), lambda i, j: (layer, 0, nc + j)),
            pl.BlockSpec((None, d, tc), lambda i, j: (layer, 0, 2 * nc + j)),
            pl.BlockSpec((None, 3, tc), lambda i, j: (layer, 0, j)),
            pl.BlockSpec((None, tc, d), lambda i, j: (layer, j, 0)),
        ],
        out_specs=pl.BlockSpec((tm, d), lambda i, j: (i, 0)),
        scratch_shapes=[pltpu.VMEM((tm + 2 * HALO, d), BF16)],
        compiler_params=_params("parallel", "arbitrary"),
        name="conv_mixer",
    )(x, x, x, g, w_in, w_in, w_in, conv_w, w_out)


def _dft_tables(seq, n1, n2, kb, cb, gdim, split_mid):
    c = np.arange(gdim)
    ang = 2.0 * np.pi * ((c[:, None] * c[None, :]) % gdim) / gdim
    fre, fim = np.cos(ang) / np.sqrt(gdim), -np.sin(ang) / np.sqrt(gdim)
    nh = gdim // cb
    fc = np.stack([np.concatenate([fre[:, q * cb:(q + 1) * cb],
                                   fim[:, q * cb:(q + 1) * cb]], axis=1)
                   for q in range(nh)])

    k = np.arange(n1)
    ang = 2.0 * np.pi * ((k[:, None] * k[None, :]) % n1) / n1
    gr, gi = np.cos(ang) / np.sqrt(n1), -np.sin(ang) / np.sqrt(n1)
    f1 = np.block([[gr, -gi], [gi, gr]])

    na = n1 // kb
    m = kb * n2
    h2 = np.zeros((na, m, 2 * m))
    nn = np.arange(n2)
    for a in range(na):
        for i in range(kb):
            k1 = a * kb + i
            kk = k1 + n1 * np.arange(n2)
            ang = 2.0 * np.pi * ((kk[:, None] * nn[None, :]) % seq) / seq
            wr, wi = np.cos(ang) / np.sqrt(n2), -np.sin(ang) / np.sqrt(n2)
            rows = np.arange(n2) * kb + i
            cols = nn * kb + i if split_mid else i * n2 + nn
            h2[a, rows[:, None], cols[None, :]] = wr
            h2[a, rows[:, None], m + cols[None, :]] = -wi
    return (np.asarray(fc, np.float32), np.asarray(f1, np.float32),
            np.asarray(h2, np.float32))


class FftCfg(NamedTuple):
    seq: int
    n1: int
    n2: int
    kb: int
    cb: int
    pitch_a: int
    pitch_o: int
    unroll: int
    h_buffers: int
    groups: int
    split_mid: bool


class _Idx(NamedTuple):
    blk: object
    span: int
    u: int

    def value(self):
        return self.blk * self.span + self.u

    def times(self, m):
        step = self.span * m
        base = self.blk * step
        if not isinstance(base, int):
            base = pl.multiple_of(base, min(step & -step, 1024))
        return base + self.u * m


def _loop(n, body, unroll):
    if unroll >= n:
        for u in range(n):
            body(_Idx(0, n, u))
    else:
        def trip(blk, carry):
            for u in range(unroll):
                body(_Idx(blk, unroll, u))
            return carry
        lax.fori_loop(0, n // unroll, trip, 0)


def _fft_body(cfg, h_ref, fc_ref, f1_ref, h2_ref, o_ref, a_ref, *b_ref):
    seq, n1, n2, kb, cb, pa, po, unroll, _, groups, split_mid = cfg
    mid_ref = b_ref[0] if split_mid else a_ref
    nl = cb // LANES
    gdim = h_ref.shape[1] // groups
    rc = max(n2, min(seq, 512))
    per = rc // n2

    def chan(r):
        for g in range(groups):
            w = jnp.dot(h_ref[pl.ds(r.times(rc), rc), g * gdim:(g + 1) * gdim],
                        fc_ref[...], preferred_element_type=F32)
            for q in range(per if pa != n2 else 1):
                size = n2 if pa != n2 else rc
                dst = pl.ds(r.times(per * pa) + q * pa, size)
                for p in range(2):
                    for s in range(nl):
                        lo = p * cb + s * LANES
                        a_ref[g, p, s, dst, :] = w[q * n2:q * n2 + size, lo:lo + LANES]
    _loop(seq // rc, chan, unroll)

    def load(ref, g, rows):
        return jnp.concatenate(
            [jnp.concatenate(
                [jnp.concatenate([ref[g, p, s, r, :] for r in rows], axis=0)
                 for s in range(nl)], axis=1)
             for p in range(2)], axis=0).astype(BF16)

    nb = max(1, 2 * LANES // cb)
    def stage1(t):
        rows = [pl.ds(t.value() * nb + q, n1, stride=pa) for q in range(nb)]
        for g in range(groups):
            rhs = jnp.concatenate([load(a_ref, g, [r]) for r in rows], axis=1)
            y = jnp.dot(f1_ref[...], rhs, preferred_element_type=F32)
            for q in range(nb):
                dst = (pl.ds(t.times(nb * n1) + q * n1, n1) if split_mid
                       else rows[q])
                for p in range(2):
                    for s in range(nl):
                        lo = q * cb + s * LANES
                        mid_ref[g, p, s, dst, :] = y[p * n1:(p + 1) * n1, lo:lo + LANES]
    _loop(n2 // nb, stage1, unroll)

    m = kb * n2
    def stage2(a):
        for g in range(groups):
            if split_mid:
                src = [pl.ds(i2 * n1 + a.times(kb), kb) for i2 in range(n2)]
            else:
                src = [pl.ds(a.times(kb * pa), m)]
            y = jnp.dot(h2_ref[a.value()], load(mid_ref, g, src),
                        preferred_element_type=F32)
            cols = slice(g * cb, (g + 1) * cb)
            if kb == 1:
                o_ref[pl.ds(a.value(), n2, stride=po), cols] = y
            else:
                for k2 in range(n2):
                    o_ref[pl.ds(a.times(kb) + k2 * po, kb), cols] = (
                        y[k2 * kb:(k2 + 1) * kb])
    _loop(n1 // kb, stage2, unroll)

    if po != n1:
        for k2 in range(n2):
            o_ref[k2 * po + n1:(k2 + 1) * po, :] = jnp.zeros(
                (po - n1, groups * cb), F32)


def _fourier(h, batch, cfg):
    t, d = h.shape
    seq, n1, n2, kb, cb, pa, po, _, _, groups, split_mid = cfg
    gdim = d // N_FGROUPS
    nh = gdim // cb
    assert kb == 1 or pa == n2 or split_mid
    assert groups == 1 or nh == 1
    assert not split_mid or kb % 8 == 0
    fc, f1, h2 = (jnp.asarray(tab).astype(BF16)
                  for tab in _dft_tables(seq, n1, n2, kb, cb, gdim, split_mid))
    out = pl.pallas_call(
        functools.partial(_fft_body, cfg),
        out_shape=jax.ShapeDtypeStruct((batch, n2 * po, d), F32),
        grid=(batch, d // (cb * groups)),
        in_specs=[
            pl.BlockSpec((None, seq, gdim * groups), lambda b, l: (b, 0, l // nh),
                         pipeline_mode=pl.Buffered(cfg.h_buffers)),
            pl.BlockSpec((None, gdim, 2 * cb), lambda b, l: (l % nh, 0, 0)),
            pl.BlockSpec(f1.shape, lambda b, l: (0, 0), pipeline_mode=pl.Buffered(1)),
            pl.BlockSpec(h2.shape, lambda b, l: (0, 0, 0), pipeline_mode=pl.Buffered(1)),
        ],
        out_specs=pl.BlockSpec((None, n2 * po, cb * groups), lambda b, l: (b, 0, l)),
        scratch_shapes=[pltpu.VMEM((groups, 2, cb // LANES, n1 * pa, LANES), F32)]
        + [pltpu.VMEM((groups, 2, cb // LANES, seq, LANES), F32)] * split_mid,
        compiler_params=_params("parallel", "arbitrary"),
        name="fourier",
    )(h.reshape(batch, seq, d), fc, f1, h2)
    return out.reshape(batch * n2 * po, d)


def _proj_body(n1, po, x_ref, m_ref, w_ref, o_ref):
    if po == n1:
        m = m_ref[...]
    else:
        m = jnp.concatenate([m_ref[q * po:q * po + n1, :]
                             for q in range(m_ref.shape[0] // po)], axis=0)
    o_ref[...] = x_ref[...] + jnp.dot(m.astype(BF16), w_ref[...],
                                      preferred_element_type=F32)


def _proj_residual(x, mixed, cfg, w, layer, *, tm=WIDE_TOKEN_TILE):
    t, d = x.shape
    mrows = tm // cfg.n1 * cfg.pitch_o
    return pl.pallas_call(
        functools.partial(_proj_body, cfg.n1, cfg.pitch_o),
        out_shape=jax.ShapeDtypeStruct((t, d), F32),
        grid=(t // tm,),
        in_specs=[pl.BlockSpec((tm, d), lambda i: (i, 0)),
                  pl.BlockSpec((mrows, d), lambda i: (i, 0)),
                  pl.BlockSpec((None, d, d), lambda i: (layer, 0, 0),
                               pipeline_mode=pl.Buffered(1))],
        out_specs=pl.BlockSpec((tm, d), lambda i: (i, 0)),
        compiler_params=_params("parallel"),
        name="proj_residual",
    )(x, mixed, w)


_FFT_CONFIG = {
    16384: FftCfg(seq=16384, n1=128, n2=128, kb=1, cb=128, pitch_a=132,
                  pitch_o=132, unroll=128, h_buffers=2, groups=1, split_mid=False),
    2048: FftCfg(seq=2048, n1=128, n2=16, kb=8, cb=256, pitch_a=20,
                 pitch_o=128, unroll=16, h_buffers=2, groups=2, split_mid=True),
}


def _trunk(x3, w):
    batch, seq, d = x3.shape
    x = x3.reshape(batch * seq, d)
    depth = w["norm_ffn"].shape[0]
    h = None
    for i in range(depth):
        j = i // 2
        if i % 2 == 0:
            x = _conv_mixer(x, seq, w["norm_mix"], w["a_w_in"], w["a_conv_w"],
                            w["a_w_out"], j)
        else:
            cfg = _FFT_CONFIG[seq]
            mixed = _fourier(h, batch, cfg)
            x = _proj_residual(x, mixed, cfg, w["f_w_out"], j)
        mlp = functools.partial(_mlp, x, w["norm_ffn"], w["w_up"], w["w_down"], i)
        if i == depth - 1:
            x = mlp("only", w["final_norm"], 0)
        elif (i + 1) % 2 == 1:
            x, h = mlp("also", w["norm_mix"], i + 1)
        else:
            x = mlp()
    return x.reshape(batch, seq, d)


def kernel(x_prompt, x_sample, norm_mix, a_w_in, a_conv_w, a_w_out, f_w_out,
           norm_ffn, w_up, w_down, final_norm):
    d = x_prompt.shape[-1]
    w = dict(
        norm_mix=norm_mix.reshape(-1, 1, d),
        norm_ffn=norm_ffn.reshape(-1, 1, d),
        final_norm=final_norm.reshape(1, 1, d),
        a_w_in=a_w_in.astype(BF16),
        a_conv_w=a_conv_w,
        a_w_out=a_w_out.astype(BF16),
        f_w_out=f_w_out.astype(BF16),
        w_up=w_up.astype(BF16),
        w_down=w_down.astype(BF16),
    )
    return _trunk(x_prompt, w), _trunk(x_sample, w)
```

```python
import functools
from typing import NamedTuple

import jax
import jax.numpy as jnp
import numpy as np
from jax import lax
from jax.experimental import pallas as pl
from jax.experimental.pallas import tpu as pltpu

F32 = jnp.float32
BF16 = jnp.bfloat16

RMS_EPS = 1e-6
N_FGROUPS = 8
LANES = 128
VMEM_BYTES_V7X = 64 * 1024 * 1024
VMEM_LIMIT_BYTES = VMEM_BYTES_V7X - 2 * 1024 * 1024

TOKEN_TILE = 512
MLP_FF_TILE = 2048
WIDE_TOKEN_TILE = 1024
CONV_CH_TILE = 512


def _params(*sem):
    return pltpu.CompilerParams(dimension_semantics=sem,
                                vmem_limit_bytes=VMEM_LIMIT_BYTES)


def _rms(x, g):
    ms = jnp.mean(x * x, axis=-1, keepdims=True)
    return (x * lax.rsqrt(ms + RMS_EPS)) * g


def _mlp_body(post, x_ref, g_ref, wu_ref, wd_ref, *rest):
    if post is None:
        acc_ref, h_ref = rest
    elif post == "also":
        g2_ref, acc_ref, n_ref, h_ref = rest
    else:
        g2_ref, n_ref, h_ref, acc_ref = rest
    j = pl.program_id(1)
    last_j = pl.num_programs(1) - 1

    def step(first, last):
        if first:
            x = x_ref[...]
            h = _rms(x, g_ref[...]).astype(BF16)
            h_ref[...] = h
        else:
            h = h_ref[...]
        u = jnp.dot(h, wu_ref[...], preferred_element_type=F32)
        u = jnp.maximum(u, 0.0)
        u = (u * u).astype(BF16)
        y = (x_ref[...] if first else acc_ref[...]) + jnp.dot(
            u, wd_ref[...], preferred_element_type=F32)
        if not (last and post == "only"):
            acc_ref[...] = y
        if last and post is not None:
            n_ref[...] = _rms(y, g2_ref[...]).astype(n_ref.dtype)

    pl.when(j == 0)(functools.partial(step, True, False))
    if post is None:
        pl.when(j > 0)(functools.partial(step, False, False))
    else:
        pl.when((j > 0) & (j < last_j))(functools.partial(step, False, False))
        pl.when(j == last_j)(functools.partial(step, False, True))


def _mlp(x, g, w_up, w_down, layer, post=None, g2=None, layer2=0, *, row0=0,
         rows=None, tm=TOKEN_TILE, tf=MLP_FF_TILE):
    d = x.shape[1]
    t = x.shape[0] if rows is None else rows
    tile0 = row0 // tm
    ff = w_up.shape[-1]
    assert ff // tf >= 2
    tile = pl.BlockSpec((tm, d), lambda i, j: (i, 0))
    in_specs = [
        pl.BlockSpec((tm, d), lambda i, j: (i + tile0, 0)),
        pl.BlockSpec((None, 1, d), lambda i, j: (layer, 0, 0)),
        pl.BlockSpec((None, d, tf), lambda i, j: (layer, 0, j)),
        pl.BlockSpec((None, tf, d), lambda i, j: (layer, j, 0)),
    ]
    args = [x, g, w_up, w_down]
    scratch = [pltpu.VMEM((tm, d), BF16)]
    x_new = jax.ShapeDtypeStruct((t, d), F32)
    if post is None:
        out_shape, out_specs = x_new, tile
    else:
        in_specs.append(pl.BlockSpec((None, 1, d), lambda i, j: (layer2, 0, 0)))
        args.append(g2)
        if post == "also":
            out_shape = (x_new, jax.ShapeDtypeStruct((t, d), BF16))
            out_specs = (tile, tile)
        else:
            out_shape, out_specs = x_new, tile
            scratch.append(pltpu.VMEM((tm, d), F32))
    return pl.pallas_call(
        functools.partial(_mlp_body, post),
        out_shape=out_shape,
        grid=(t // tm, ff // tf),
        in_specs=in_specs,
        out_specs=out_specs,
        scratch_shapes=scratch,
        compiler_params=_params("parallel", "arbitrary"),
        name="mlp",
    )(*args)


HALO = 8


def _seq_edge(t, segs):
    hit = False
    for s0, s1, seq in segs:
        hit = hit | ((t >= s0) & (t <= s1) & ((t - s0) % seq == 0))
    return hit


def _conv_body(segs, tile0, tm, x_ref, xp_ref, xn_ref, g_ref, wb_ref, wc_ref,
               wv_ref, cw_ref, wo_ref, *rest):
    o_ref, h_ref = rest[-2:]
    i = pl.program_id(0)
    j = pl.program_id(1)
    rows = tm + 2 * HALO
    t0 = (i + tile0) * tm

    def step(first):
        if first:
            g = g_ref[...]
            keep_prev = jnp.logical_not(_seq_edge(t0, segs)).astype(F32)
            keep_next = jnp.logical_not(_seq_edge(t0 + tm, segs)).astype(F32)
            h = jnp.concatenate(
                [_rms(x_ref[...], g),
                 _rms(xn_ref[...], g) * keep_next,
                 _rms(xp_ref[...], g) * keep_prev], axis=0).astype(BF16)
            h_ref[...] = h
        else:
            h = h_ref[...]
        c = jnp.dot(h, wc_ref[...], preferred_element_type=F32)
        v = jnp.dot(h, wv_ref[...], preferred_element_type=F32)
        b = jnp.dot(h[:tm], wb_ref[...], preferred_element_type=F32)
        u = c * v
        cw = cw_ref[...]
        u_prev = pltpu.roll(u, 1, axis=0)[:tm]
        u_next = pltpu.roll(u, rows - 1, axis=0)[:tm]
        conv = cw[0:1] * u_prev + cw[1:2] * u[:tm] + cw[2:3] * u_next
        gated = (b * conv).astype(BF16)
        o_ref[...] = (x_ref[...] if first else o_ref[...]) + jnp.dot(
            gated, wo_ref[...], preferred_element_type=F32)

    pl.when(j == 0)(functools.partial(step, True))
    pl.when(j > 0)(functools.partial(step, False))


def _conv_mixer(x, segs, g, w_in, conv_w, w_out, layer, *, out_rows, row0=0,
                buf=None, tm=WIDE_TOKEN_TILE, tc=CONV_CH_TILE):
    t, d = x.shape
    cdim = conv_w.shape[-1]
    nc = cdim // tc
    hb = tm // HALO
    last = t // HALO - 1
    tile0 = row0 // tm
    in_specs = [
        pl.BlockSpec((tm, d), lambda i, j: (i, 0)),
        pl.BlockSpec((HALO, d), lambda i, j: (jnp.maximum(i * hb - 1, 0), 0)),
        pl.BlockSpec((HALO, d), lambda i, j: (jnp.minimum((i + 1) * hb, last), 0)),
        pl.BlockSpec((None, 1, d), lambda i, j: (2 * layer, 0, 0)),
        pl.BlockSpec((None, d, tc), lambda i, j: (layer, 0, j)),
        pl.BlockSpec((None, d, tc), lambda i, j: (layer, 0, nc + j)),
        pl.BlockSpec((None, d, tc), lambda i, j: (layer, 0, 2 * nc + j)),
        pl.BlockSpec((None, 3, tc), lambda i, j: (layer, 0, j)),
        pl.BlockSpec((None, tc, d), lambda i, j: (layer, j, 0)),
    ]
    args = [x, x, x, g, w_in, w_in, w_in, conv_w, w_out]
    aliases = {}
    if buf is not None:
        aliases = {len(args): 0}
        in_specs.append(pl.BlockSpec(memory_space=pl.ANY))
        args.append(buf)
    return pl.pallas_call(
        functools.partial(_conv_body, segs, tile0, tm),
        out_shape=jax.ShapeDtypeStruct((out_rows, d), F32),
        grid=(t // tm, nc),
        in_specs=in_specs,
        out_specs=pl.BlockSpec((tm, d), lambda i, j: (i + tile0, 0)),
        scratch_shapes=[pltpu.VMEM((tm + 2 * HALO, d), BF16)],
        input_output_aliases=aliases,
        compiler_params=_params("parallel", "arbitrary"),
        name="conv_mixer",
    )(*args)


def _dft_tables(seq, n1, n2, kb, cb, gdim, split_mid):
    c = np.arange(gdim)
    ang = 2.0 * np.pi * ((c[:, None] * c[None, :]) % gdim) / gdim
    fre, fim = np.cos(ang) / np.sqrt(gdim), -np.sin(ang) / np.sqrt(gdim)
    nh = gdim // cb
    fc = np.stack([np.concatenate([fre[:, q * cb:(q + 1) * cb],
                                   fim[:, q * cb:(q + 1) * cb]], axis=1)
                   for q in range(nh)])

    k = np.arange(n1)
    ang = 2.0 * np.pi * ((k[:, None] * k[None, :]) % n1) / n1
    gr, gi = np.cos(ang) / np.sqrt(n1), -np.sin(ang) / np.sqrt(n1)
    f1 = np.block([[gr, -gi], [gi, gr]])

    na = n1 // kb
    m = kb * n2
    h2 = np.zeros((na, m, 2 * m))
    nn = np.arange(n2)
    for a in range(na):
        for i in range(kb):
            k1 = a * kb + i
            kk = k1 + n1 * np.arange(n2)
            ang = 2.0 * np.pi * ((kk[:, None] * nn[None, :]) % seq) / seq
            wr, wi = np.cos(ang) / np.sqrt(n2), -np.sin(ang) / np.sqrt(n2)
            rows = np.arange(n2) * kb + i
            cols = nn * kb + i if split_mid else i * n2 + nn
            h2[a, rows[:, None], cols[None, :]] = wr
            h2[a, rows[:, None], m + cols[None, :]] = -wi
    return (np.asarray(fc, np.float32), np.asarray(f1, np.float32),
            np.asarray(h2, np.float32))


class FftCfg(NamedTuple):
    seq: int
    n1: int
    n2: int
    kb: int
    cb: int
    pitch_a: int
    pitch_o: int
    unroll: int
    h_buffers: int
    groups: int
    split_mid: bool


class _Idx(NamedTuple):
    blk: object
    span: int
    u: int

    def value(self):
        return self.blk * self.span + self.u

    def times(self, m):
        step = self.span * m
        base = self.blk * step
        if not isinstance(base, int):
            base = pl.multiple_of(base, min(step & -step, 1024))
        return base + self.u * m


def _loop(n, body, unroll):
    if unroll >= n:
        for u in range(n):
            body(_Idx(0, n, u))
    else:
        def trip(blk, carry):
            for u in range(unroll):
                body(_Idx(blk, unroll, u))
            return carry
        lax.fori_loop(0, n // unroll, trip, 0)


def _fft_body(cfg, h_ref, fc_ref, f1_ref, h2_ref, o_ref, a_ref, *b_ref):
    seq, n1, n2, kb, cb, pa, po, unroll, _, groups, split_mid = cfg
    mid_ref = b_ref[0] if split_mid else a_ref
    nl = cb // LANES
    gdim = h_ref.shape[1] // groups
    rc = max(n2, min(seq, 512))
    per = rc // n2

    def chan(r):
        for g in range(groups):
            w = jnp.dot(h_ref[pl.ds(r.times(rc), rc), g * gdim:(g + 1) * gdim],
                        fc_ref[...], preferred_element_type=F32)
            for q in range(per if pa != n2 else 1):
                size = n2 if pa != n2 else rc
                dst = pl.ds(r.times(per * pa) + q * pa, size)
                for p in range(2):
                    for s in range(nl):
                        lo = p * cb + s * LANES
                        a_ref[g, p, s, dst, :] = w[q * n2:q * n2 + size, lo:lo + LANES]
    _loop(seq // rc, chan, unroll)

    def load(ref, g, rows):
        return jnp.concatenate(
            [jnp.concatenate(
                [jnp.concatenate([ref[g, p, s, r, :] for r in rows], axis=0)
                 for s in range(nl)], axis=1)
             for p in range(2)], axis=0).astype(BF16)

    nb = max(1, 2 * LANES // cb)
    def stage1(t):
        rows = [pl.ds(t.value() * nb + q, n1, stride=pa) for q in range(nb)]
        for g in range(groups):
            rhs = jnp.concatenate([load(a_ref, g, [r]) for r in rows], axis=1)
            y = jnp.dot(f1_ref[...], rhs, preferred_element_type=F32)
            for q in range(nb):
                dst = (pl.ds(t.times(nb * n1) + q * n1, n1) if split_mid
                       else rows[q])
                for p in range(2):
                    for s in range(nl):
                        lo = q * cb + s * LANES
                        mid_ref[g, p, s, dst, :] = y[p * n1:(p + 1) * n1, lo:lo + LANES]
    _loop(n2 // nb, stage1, unroll)

    m = kb * n2
    def stage2(a):
        for g in range(groups):
            if split_mid:
                src = [pl.ds(i2 * n1 + a.times(kb), kb) for i2 in range(n2)]
            else:
                src = [pl.ds(a.times(kb * pa), m)]
            y = jnp.dot(h2_ref[a.value()], load(mid_ref, g, src),
                        preferred_element_type=F32)
            cols = slice(g * cb, (g + 1) * cb)
            if kb == 1:
                o_ref[pl.ds(a.value(), n2, stride=po), cols] = y
            else:
                for k2 in range(n2):
                    o_ref[pl.ds(a.times(kb) + k2 * po, kb), cols] = (
                        y[k2 * kb:(k2 + 1) * kb])
    _loop(n1 // kb, stage2, unroll)

    if po != n1:
        for k2 in range(n2):
            o_ref[k2 * po + n1:(k2 + 1) * po, :] = jnp.zeros(
                (po - n1, groups * cb), F32)


def _fourier(h, batch, cfg, batch0=0):
    t, d = h.shape
    seq, n1, n2, kb, cb, pa, po, _, _, groups, split_mid = cfg
    gdim = d // N_FGROUPS
    nh = gdim // cb
    assert kb == 1 or pa == n2 or split_mid
    assert groups == 1 or nh == 1
    assert not split_mid or kb % 8 == 0
    fc, f1, h2 = (jnp.asarray(tab).astype(BF16)
                  for tab in _dft_tables(seq, n1, n2, kb, cb, gdim, split_mid))
    out = pl.pallas_call(
        functools.partial(_fft_body, cfg),
        out_shape=jax.ShapeDtypeStruct((batch, n2 * po, d), F32),
        grid=(batch, d // (cb * groups)),
        in_specs=[
            pl.BlockSpec((None, seq, gdim * groups),
                         lambda b, l: (b + batch0, 0, l // nh),
                         pipeline_mode=pl.Buffered(cfg.h_buffers)),
            pl.BlockSpec((None, gdim, 2 * cb), lambda b, l: (l % nh, 0, 0)),
            pl.BlockSpec(f1.shape, lambda b, l: (0, 0), pipeline_mode=pl.Buffered(1)),
            pl.BlockSpec(h2.shape, lambda b, l: (0, 0, 0), pipeline_mode=pl.Buffered(1)),
        ],
        out_specs=pl.BlockSpec((None, n2 * po, cb * groups), lambda b, l: (b, 0, l)),
        scratch_shapes=[pltpu.VMEM((groups, 2, cb // LANES, n1 * pa, LANES), F32)]
        + [pltpu.VMEM((groups, 2, cb // LANES, seq, LANES), F32)] * split_mid,
        compiler_params=_params("parallel", "arbitrary"),
        name="fourier",
    )(h.reshape(t // seq, seq, d), fc, f1, h2)
    return out.reshape(batch * n2 * po, d)


def _proj_body(n1, po, x_ref, m_ref, w_ref, o_ref):
    if po == n1:
        m = m_ref[...]
    else:
        m = jnp.concatenate([m_ref[q * po:q * po + n1, :]
                             for q in range(m_ref.shape[0] // po)], axis=0)
    o_ref[...] = x_ref[...] + jnp.dot(m.astype(BF16), w_ref[...],
                                      preferred_element_type=F32)


def _proj_residual(x, mixed, cfg, w, layer, *, row0, rows, tm=WIDE_TOKEN_TILE):
    d = x.shape[1]
    tile0 = row0 // tm
    mrows = tm // cfg.n1 * cfg.pitch_o
    return pl.pallas_call(
        functools.partial(_proj_body, cfg.n1, cfg.pitch_o),
        out_shape=jax.ShapeDtypeStruct(x.shape, F32),
        grid=(rows // tm,),
        in_specs=[pl.BlockSpec((tm, d), lambda i: (i + tile0, 0)),
                  pl.BlockSpec((mrows, d), lambda i: (i, 0)),
                  pl.BlockSpec((None, d, d), lambda i: (layer, 0, 0),
                               pipeline_mode=pl.Buffered(1))],
        out_specs=pl.BlockSpec((tm, d), lambda i: (i + tile0, 0)),
        input_output_aliases={0: 0},
        compiler_params=_params("parallel"),
        name="proj_residual",
    )(x, mixed, w)


_FFT_CONFIG = {
    16384: FftCfg(seq=16384, n1=128, n2=128, kb=1, cb=128, pitch_a=132,
                  pitch_o=132, unroll=128, h_buffers=2, groups=1, split_mid=False),
    2048: FftCfg(seq=2048, n1=128, n2=16, kb=8, cb=256, pitch_a=20,
                 pitch_o=128, unroll=16, h_buffers=2, groups=2, split_mid=True),
}


def _trunks(inputs, w):
    d = inputs[0].shape[-1]
    parts, row0 = [], 0
    for x3 in inputs:
        batch, seq, _ = x3.shape
        parts.append((x3.reshape(batch * seq, d), row0, batch * seq, batch, seq))
        row0 += batch * seq
    total = row0
    segs = tuple((r0, r0 + rows, seq) for _, r0, rows, _, seq in parts)
    depth = w["norm_ffn"].shape[0]
    conv = functools.partial(_conv_mixer, g=w["norm_mix"], w_in=w["a_w_in"],
                             conv_w=w["a_conv_w"], w_out=w["a_w_out"],
                             out_rows=total)
    x = None
    h = None
    for i in range(depth):
        j = i // 2
        if i % 2 == 0 and x is None:
            for flat, r0, _, _, _ in parts:
                x = conv(flat, segs, layer=j, row0=r0, buf=x)
        elif i % 2 == 0:
            x = conv(x, segs, layer=j)
        else:
            for _, r0, rows, batch, seq in parts:
                cfg = _FFT_CONFIG[seq]
                mixed = _fourier(h, batch, cfg, r0 // seq)
                x = _proj_residual(x, mixed, cfg, w["f_w_out"], j, row0=r0, rows=rows)
        mlp = functools.partial(_mlp, x, w["norm_ffn"], w["w_up"], w["w_down"], i)
        if i == depth - 1:
            return tuple(
                mlp("only", w["final_norm"], 0, row0=r0, rows=rows).reshape(batch, seq, d)
                for _, r0, rows, batch, seq in parts)
        if (i + 1) % 2 == 1:
            x, h = mlp("also", w["norm_mix"], i + 1)
        else:
            x = mlp()


def kernel(x_prompt, x_sample, norm_mix, a_w_in, a_conv_w, a_w_out, f_w_out,
           norm_ffn, w_up, w_down, final_norm):
    d = x_prompt.shape[-1]
    w = dict(
        norm_mix=norm_mix.reshape(-1, 1, d),
        norm_ffn=norm_ffn.reshape(-1, 1, d),
        final_norm=final_norm.reshape(1, 1, d),
        a_w_in=a_w_in.astype(BF16),
        a_conv_w=a_conv_w,
        a_w_out=a_w_out.astype(BF16),
        f_w_out=f_w_out.astype(BF16),
        w_up=w_up.astype(BF16),
        w_down=w_down.astype(BF16),
    )
    return _trunks((x_prompt, x_sample), w)
```

```python
import functools
from typing import NamedTuple

import jax
import jax.numpy as jnp
import numpy as np
from jax import lax
from jax.experimental import pallas as pl
from jax.experimental.pallas import tpu as pltpu

F32 = jnp.float32
BF16 = jnp.bfloat16

RMS_EPS = 1e-6
N_FGROUPS = 8
LANES = 128
VMEM_BYTES_V7X = 64 * 1024 * 1024
VMEM_LIMIT_BYTES = VMEM_BYTES_V7X - 2 * 1024 * 1024

TOKEN_TILE = 512
MLP_FF_TILE = 2048
WIDE_TOKEN_TILE = 1024
CONV_CH_TILE = 512


def _params(*sem):
    return pltpu.CompilerParams(dimension_semantics=sem,
                                vmem_limit_bytes=VMEM_LIMIT_BYTES)


def _rms(x, g):
    ms = jnp.mean(x * x, axis=-1, keepdims=True)
    return (x * lax.rsqrt(ms + RMS_EPS)) * g


def _mlp_body(post, x_ref, g_ref, wu_ref, wd_ref, *rest):
    if post is None:
        acc_ref, h_ref = rest
    elif post == "also":
        g2_ref, acc_ref, n_ref, h_ref = rest
    else:
        g2_ref, n_ref, h_ref, acc_ref = rest
    j = pl.program_id(1)
    last_j = pl.num_programs(1) - 1

    def step(first, last):
        if first:
            x = x_ref[...]
            h = _rms(x, g_ref[...]).astype(BF16)
            h_ref[...] = h
        else:
            h = h_ref[...]
        u = jnp.dot(h, wu_ref[...], preferred_element_type=F32)
        u = jnp.maximum(u, 0.0)
        u = (u * u).astype(BF16)
        y = (x_ref[...] if first else acc_ref[...]) + jnp.dot(
            u, wd_ref[...], preferred_element_type=F32)
        if not (last and post == "only"):
            acc_ref[...] = y
        if last and post is not None:
            n_ref[...] = _rms(y, g2_ref[...]).astype(n_ref.dtype)

    pl.when(j == 0)(functools.partial(step, True, False))
    if post is None:
        pl.when(j > 0)(functools.partial(step, False, False))
    else:
        pl.when((j > 0) & (j < last_j))(functools.partial(step, False, False))
        pl.when(j == last_j)(functools.partial(step, False, True))


def _mlp(x, g, w_up, w_down, layer, post=None, g2=None, layer2=0, *, tm=TOKEN_TILE,
         tf=MLP_FF_TILE):
    t, d = x.shape
    ff = w_up.shape[-1]
    assert ff // tf >= 2
    tile = pl.BlockSpec((tm, d), lambda i, j: (i, 0))
    in_specs = [
        tile,
        pl.BlockSpec((None, 1, d), lambda i, j: (layer, 0, 0)),
        pl.BlockSpec((None, d, tf), lambda i, j: (layer, 0, j)),
        pl.BlockSpec((None, tf, d), lambda i, j: (layer, j, 0)),
    ]
    args = [x, g, w_up, w_down]
    scratch = [pltpu.VMEM((tm, d), BF16)]
    x_new = jax.ShapeDtypeStruct((t, d), F32)
    if post is None:
        out_shape, out_specs = x_new, tile
    else:
        in_specs.append(pl.BlockSpec((None, 1, d), lambda i, j: (layer2, 0, 0)))
        args.append(g2)
        if post == "also":
            out_shape = (x_new, jax.ShapeDtypeStruct((t, d), BF16))
            out_specs = (tile, tile)
        else:
            out_shape, out_specs = x_new, tile
            scratch.append(pltpu.VMEM((tm, d), F32))
    return pl.pallas_call(
        functools.partial(_mlp_body, post),
        out_shape=out_shape,
        grid=(t // tm, ff // tf),
        in_specs=in_specs,
        out_specs=out_specs,
        scratch_shapes=scratch,
        compiler_params=_params("parallel", "arbitrary"),
        name="mlp",
    )(*args)


HALO = 8


def _conv_body(seq, tm, x_ref, xp_ref, xn_ref, g_ref, wb_ref, wc_ref, wv_ref,
               cw_ref, wo_ref, o_ref, h_ref):
    i = pl.program_id(0)
    j = pl.program_id(1)
    rows = tm + 2 * HALO

    def step(first):
        if first:
            g = g_ref[...]
            keep_prev = ((i * tm) % seq != 0).astype(F32)
            keep_next = (((i + 1) * tm) % seq != 0).astype(F32)
            h = jnp.concatenate(
                [_rms(x_ref[...], g),
                 _rms(xn_ref[...], g) * keep_next,
                 _rms(xp_ref[...], g) * keep_prev], axis=0).astype(BF16)
            h_ref[...] = h
        else:
            h = h_ref[...]
        c = jnp.dot(h, wc_ref[...], preferred_element_type=F32)
        v = jnp.dot(h, wv_ref[...], preferred_element_type=F32)
        b = jnp.dot(h[:tm], wb_ref[...], preferred_element_type=F32)
        u = c * v
        cw = cw_ref[...]
        u_prev = pltpu.roll(u, 1, axis=0)[:tm]
        u_next = pltpu.roll(u, rows - 1, axis=0)[:tm]
        conv = cw[0:1] * u_prev + cw[1:2] * u[:tm] + cw[2:3] * u_next
        gated = (b * conv).astype(BF16)
        o_ref[...] = (x_ref[...] if first else o_ref[...]) + jnp.dot(
            gated, wo_ref[...], preferred_element_type=F32)

    pl.when(j == 0)(functools.partial(step, True))
    pl.when(j > 0)(functools.partial(step, False))


def _conv_mixer(x, seq, g, w_in, conv_w, w_out, layer, *, tm=WIDE_TOKEN_TILE,
                tc=CONV_CH_TILE):
    t, d = x.shape
    cdim = conv_w.shape[-1]
    nc = cdim // tc
    hb = tm // HALO
    last = t // HALO - 1
    return pl.pallas_call(
        functools.partial(_conv_body, seq, tm),
        out_shape=jax.ShapeDtypeStruct((t, d), F32),
        grid=(t // tm, nc),
        in_specs=[
            pl.BlockSpec((tm, d), lambda i, j: (i, 0)),
            pl.BlockSpec((HALO, d), lambda i, j: (jnp.maximum(i * hb - 1, 0), 0)),
            pl.BlockSpec((HALO, d), lambda i, j: (jnp.minimum((i + 1) * hb, last), 0)),
            pl.BlockSpec((None, 1, d), lambda i, j: (2 * layer, 0, 0)),
            pl.BlockSpec((None, d, tc), lambda i, j: (layer, 0, j)),
            pl.BlockSpec((None, d, tc), lambda i, j: (layer, 0, nc + j)),
            pl.BlockSpec((None, d, tc), lambda i, j: (layer, 0, 2 * nc + j)),
            pl.BlockSpec((None, 3, tc), lambda i, j: (layer, 0, j)),
            pl.BlockSpec((None, tc, d), lambda i, j: (layer, j, 0)),
        ],
        out_specs=pl.BlockSpec((tm, d), lambda i, j: (i, 0)),
        scratch_shapes=[pltpu.VMEM((tm + 2 * HALO, d), BF16)],
        compiler_params=_params("parallel", "arbitrary"),
        name="conv_mixer",
    )(x, x, x, g, w_in, w_in, w_in, conv_w, w_out)


def _dft_tables(seq, n1, n2, kb, cb, gdim, split_mid):
    c = np.arange(gdim)
    ang = 2.0 * np.pi * ((c[:, None] * c[None, :]) % gdim) / gdim
    fre, fim = np.cos(ang) / np.sqrt(gdim), -np.sin(ang) / np.sqrt(gdim)
    nh = gdim // cb
    fc = np.stack([np.concatenate([fre[:, q * cb:(q + 1) * cb],
                                   fim[:, q * cb:(q + 1) * cb]], axis=1)
                   for q in range(nh)])

    k = np.arange(n1)
    ang = 2.0 * np.pi * ((k[:, None] * k[None, :]) % n1) / n1
    gr, gi = np.cos(ang) / np.sqrt(n1), -np.sin(ang) / np.sqrt(n1)
    f1 = np.block([[gr, -gi], [gi, gr]])

    na = n1 // kb
    m = kb * n2
    h2 = np.zeros((na, m, 2 * m))
    nn = np.arange(n2)
    for a in range(na):
        for i in range(kb):
            k1 = a * kb + i
            kk = k1 + n1 * np.arange(n2)
            ang = 2.0 * np.pi * ((kk[:, None] * nn[None, :]) % seq) / seq
            wr, wi = np.cos(ang) / np.sqrt(n2), -np.sin(ang) / np.sqrt(n2)
            rows = np.arange(n2) * kb + i
            cols = nn * kb + i if split_mid else i * n2 + nn
            h2[a, rows[:, None], cols[None, :]] = wr
            h2[a, rows[:, None], m + cols[None, :]] = -wi
    return (np.asarray(fc, np.float32), np.asarray(f1, np.float32),
            np.asarray(h2, np.float32))


class FftCfg(NamedTuple):
    seq: int
    n1: int
    n2: int
    kb: int
    cb: int
    pitch_a: int
    pitch_o: int
    unroll: int
    h_buffers: int
    groups: int
    split_mid: bool
    passes: int


class _Idx(NamedTuple):
    blk: object
    span: int
    u: int

    def value(self):
        return self.blk * self.span + self.u

    def times(self, m):
        step = self.span * m
        base = self.blk * step
        if not isinstance(base, int):
            base = pl.multiple_of(base, min(step & -step, 1024))
        return base + self.u * m


def _loop(n, body, unroll):
    if unroll >= n:
        for u in range(n):
            body(_Idx(0, n, u))
    else:
        def trip(blk, carry):
            for u in range(unroll):
                body(_Idx(blk, unroll, u))
            return carry
        lax.fori_loop(0, n // unroll, trip, 0)


def _fft_body(cfg, h_ref, fc_ref, f1_ref, h2_ref, o_ref, a_ref, *b_ref):
    seq, n1, n2, kb, cb, pa, po, unroll, _, groups, split_mid, passes = cfg
    mid_ref = b_ref[0] if split_mid else a_ref
    nl = cb // LANES
    gdim = h_ref.shape[1] // groups
    gpp = groups // passes
    rc = max(n2, min(seq, 512))
    per = rc // n2

    def chan(g0, r):
        for g in range(gpp):
            w = jnp.dot(h_ref[pl.ds(r.times(rc), rc), (g0 + g) * gdim:(g0 + g + 1) * gdim],
                        fc_ref[...], preferred_element_type=F32)
            for q in range(per if pa != n2 else 1):
                size = n2 if pa != n2 else rc
                dst = pl.ds(r.times(per * pa) + q * pa, size)
                for p in range(2):
                    for s in range(nl):
                        lo = p * cb + s * LANES
                        a_ref[g, p, s, dst, :] = w[q * n2:q * n2 + size, lo:lo + LANES]

    def load(ref, g, rows):
        return jnp.concatenate(
            [jnp.concatenate(
                [jnp.concatenate([ref[g, p, s, r, :] for r in rows], axis=0)
                 for s in range(nl)], axis=1)
             for p in range(2)], axis=0).astype(BF16)

    nb = max(1, 2 * LANES // cb)
    def stage1(t):
        rows = [pl.ds(t.value() * nb + q, n1, stride=pa) for q in range(nb)]
        for g in range(gpp):
            rhs = jnp.concatenate([load(a_ref, g, [r]) for r in rows], axis=1)
            y = jnp.dot(f1_ref[...], rhs, preferred_element_type=F32)
            for q in range(nb):
                dst = (pl.ds(t.times(nb * n1) + q * n1, n1) if split_mid
                       else rows[q])
                for p in range(2):
                    for s in range(nl):
                        lo = q * cb + s * LANES
                        mid_ref[g, p, s, dst, :] = y[p * n1:(p + 1) * n1, lo:lo + LANES]

    m = kb * n2
    def stage2(g0, a):
        for g in range(gpp):
            if split_mid:
                src = [pl.ds(i2 * n1 + a.times(kb), kb) for i2 in range(n2)]
            else:
                src = [pl.ds(a.times(kb * pa), m)]
            y = jnp.dot(h2_ref[a.value()], load(mid_ref, g, src),
                        preferred_element_type=F32)
            cols = slice((g0 + g) * cb, (g0 + g + 1) * cb)
            if kb == 1:
                o_ref[pl.ds(a.value(), n2, stride=po), cols] = y
            else:
                for k2 in range(n2):
                    o_ref[pl.ds(a.times(kb) + k2 * po, kb), cols] = (
                        y[k2 * kb:(k2 + 1) * kb])

    for g0 in range(0, groups, gpp):
        _loop(seq // rc, functools.partial(chan, g0), unroll)
        _loop(n2 // nb, stage1, unroll)
        _loop(n1 // kb, functools.partial(stage2, g0), unroll)

    if po != n1:
        for k2 in range(n2):
            o_ref[k2 * po + n1:(k2 + 1) * po, :] = jnp.zeros(
                (po - n1, groups * cb), F32)


def _fourier(h, batch, cfg):
    t, d = h.shape
    seq, n1, n2, kb, cb, pa, po, _, _, groups, split_mid, passes = cfg
    gdim = d // N_FGROUPS
    gpp = groups // passes
    nh = gdim // cb
    assert kb == 1 or pa == n2 or split_mid
    assert groups == 1 or nh == 1
    assert not split_mid or kb % 8 == 0
    fc, f1, h2 = (jnp.asarray(tab).astype(BF16)
                  for tab in _dft_tables(seq, n1, n2, kb, cb, gdim, split_mid))
    out = pl.pallas_call(
        functools.partial(_fft_body, cfg),
        out_shape=jax.ShapeDtypeStruct((batch, n2 * po, d), F32),
        grid=(batch, d // (cb * groups)),
        in_specs=[
            pl.BlockSpec((None, seq, gdim * groups), lambda b, l: (b, 0, l // nh),
                         pipeline_mode=pl.Buffered(cfg.h_buffers)),
            pl.BlockSpec((None, gdim, 2 * cb), lambda b, l: (l % nh, 0, 0)),
            pl.BlockSpec(f1.shape, lambda b, l: (0, 0), pipeline_mode=pl.Buffered(1)),
            pl.BlockSpec(h2.shape, lambda b, l: (0, 0, 0), pipeline_mode=pl.Buffered(1)),
        ],
        out_specs=pl.BlockSpec((None, n2 * po, cb * groups), lambda b, l: (b, 0, l)),
        scratch_shapes=[pltpu.VMEM((gpp, 2, cb // LANES, n1 * pa, LANES), F32)]
        + [pltpu.VMEM((gpp, 2, cb // LANES, seq, LANES), F32)] * split_mid,
        compiler_params=_params("parallel", "arbitrary"),
        name="fourier",
    )(h.reshape(batch, seq, d), fc, f1, h2)
    return out.reshape(batch * n2 * po, d)


def _proj_body(n1, po, x_ref, m_ref, w_ref, o_ref):
    if po == n1:
        m = m_ref[...]
    else:
        m = jnp.concatenate([m_ref[q * po:q * po + n1, :]
                             for q in range(m_ref.shape[0] // po)], axis=0)
    o_ref[...] = x_ref[...] + jnp.dot(m.astype(BF16), w_ref[...],
                                      preferred_element_type=F32)


def _proj_residual(x, mixed, cfg, w, layer, *, tm=WIDE_TOKEN_TILE):
    t, d = x.shape
    mrows = tm // cfg.n1 * cfg.pitch_o
    return pl.pallas_call(
        functools.partial(_proj_body, cfg.n1, cfg.pitch_o),
        out_shape=jax.ShapeDtypeStruct((t, d), F32),
        grid=(t // tm,),
        in_specs=[pl.BlockSpec((tm, d), lambda i: (i, 0)),
                  pl.BlockSpec((mrows, d), lambda i: (i, 0)),
                  pl.BlockSpec((None, d, d), lambda i: (layer, 0, 0),
                               pipeline_mode=pl.Buffered(1))],
        out_specs=pl.BlockSpec((tm, d), lambda i: (i, 0)),
        compiler_params=_params("parallel"),
        name="proj_residual",
    )(x, mixed, w)


_FFT_CONFIG = {
    16384: FftCfg(seq=16384, n1=128, n2=128, kb=1, cb=128, pitch_a=132,
                  pitch_o=132, unroll=128, h_buffers=2, groups=1, split_mid=False,
                  passes=1),
    2048: FftCfg(seq=2048, n1=128, n2=16, kb=8, cb=256, pitch_a=20,
                 pitch_o=128, unroll=16, h_buffers=2, groups=4, split_mid=True,
                 passes=2),
}


def _trunk(x3, w):
    batch, seq, d = x3.shape
    x = x3.reshape(batch * seq, d)
    depth = w["norm_ffn"].shape[0]
    h = None
    for i in range(depth):
        j = i // 2
        if i % 2 == 0:
            x = _conv_mixer(x, seq, w["norm_mix"], w["a_w_in"], w["a_conv_w"],
                            w["a_w_out"], j)
        else:
            cfg = _FFT_CONFIG[seq]
            mixed = _fourier(h, batch, cfg)
            x = _proj_residual(x, mixed, cfg, w["f_w_out"], j)
        mlp = functools.partial(_mlp, x, w["norm_ffn"], w["w_up"], w["w_down"], i)
        if i == depth - 1:
            x = mlp("only", w["final_norm"], 0)
        elif (i + 1) % 2 == 1:
            x, h = mlp("also", w["norm_mix"], i + 1)
        else:
            x = mlp()
    return x.reshape(batch, seq, d)


def kernel(x_prompt, x_sample, norm_mix, a_w_in, a_conv_w, a_w_out, f_w_out,
           norm_ffn, w_up, w_down, final_norm):
    d = x_prompt.shape[-1]
    w = dict(
        norm_mix=norm_mix.reshape(-1, 1, d),
        norm_ffn=norm_ffn.reshape(-1, 1, d),
        final_norm=final_norm.reshape(1, 1, d),
        a_w_in=a_w_in.astype(BF16),
        a_conv_w=a_conv_w,
        a_w_out=a_w_out.astype(BF16),
        f_w_out=f_w_out.astype(BF16),
        w_up=w_up.astype(BF16),
        w_down=w_down.astype(BF16),
    )
    return _trunk(x_prompt, w), _trunk(x_sample, w)
```
